```python
import math
import jax, jax.numpy as jnp
from jax import lax
import numpy as np

D_MODEL = 1024
BATCH = 8
SEQ = 4096
DEPTH = 1

CHUNK = 64
Q_BLOCK = 128
ATTN_HEADS = 4
ATTN_QK_DIM = 64
ATTN_V_DIM = 2 * ATTN_QK_DIM
ATTN_WIDTH = ATTN_HEADS * ATTN_V_DIM
ATTN_QK_WIDTH = ATTN_HEADS * 2 * ATTN_QK_DIM
SSM_GROUP = 16
SSM_STATE = 64
SSM_WIDTH = D_MODEL // 2
SSM_GROUPS = SSM_WIDTH // SSM_GROUP
N_BRANCH = 2
D_FF = 4 * D_MODEL
IN_COLS = 2 * ATTN_QK_WIDTH + ATTN_WIDTH + SSM_WIDTH + N_BRANCH * D_MODEL
EPS = 1e-6
DT_MIN = 1e-3
DT_MAX = 1e-1

kernel_name = "gated_diffattn_s5_hybrid_block"


def rms_norm(x, g):
    xf = x.astype(jnp.float32)
    y = xf * lax.rsqrt(jnp.mean(xf * xf, axis=-1, keepdims=True) + EPS)
    return (y * g.astype(jnp.float32)).astype(x.dtype)


def lambda_init_fn(layer_idx):
    return 0.8 - 0.6 * math.exp(-0.3 * layer_idx)


def diff_attention(q, k, v, q_norm_g, k_norm_g, lambda_q1, lambda_k1, lambda_q2, lambda_k2,
                   subln_g, layer_idx):
    bsz, seq, _ = q.shape
    dtype = q.dtype
    q = rms_norm(q.reshape(bsz, seq, ATTN_HEADS, 2, ATTN_QK_DIM), q_norm_g)
    k = rms_norm(k.reshape(bsz, seq, ATTN_HEADS, 2, ATTN_QK_DIM), k_norm_g)
    q1 = jnp.transpose(q[:, :, :, 0], (0, 2, 1, 3))
    q2 = jnp.transpose(q[:, :, :, 1], (0, 2, 1, 3))
    k1 = jnp.transpose(k[:, :, :, 0], (0, 2, 1, 3))
    k2 = jnp.transpose(k[:, :, :, 1], (0, 2, 1, 3))
    vh = jnp.transpose(v.reshape(bsz, seq, ATTN_HEADS, ATTN_V_DIM), (0, 2, 1, 3))

    lam_init = lambda_init_fn(layer_idx)
    lam = (jnp.exp(jnp.sum(lambda_q1.astype(jnp.float32) * lambda_k1.astype(jnp.float32)))
           - jnp.exp(jnp.sum(lambda_q2.astype(jnp.float32) * lambda_k2.astype(jnp.float32)))
           + lam_init)
    scale = ATTN_QK_DIM ** -0.5
    k_chunk = jnp.arange(seq) // CHUNK
    n_blocks = seq // Q_BLOCK

    def one_block(i):
        start = i * Q_BLOCK
        q1b = lax.dynamic_slice_in_dim(q1, start, Q_BLOCK, axis=2)
        q2b = lax.dynamic_slice_in_dim(q2, start, Q_BLOCK, axis=2)
        q_chunk = (start + jnp.arange(Q_BLOCK)) // CHUNK
        allowed = k_chunk[None, :] <= q_chunk[:, None]
        s1 = jnp.einsum('bhqd,bhkd->bhqk', q1b, k1).astype(jnp.float32) * scale
        s2 = jnp.einsum('bhqd,bhkd->bhqk', q2b, k2).astype(jnp.float32) * scale
        p1 = jax.nn.softmax(jnp.where(allowed, s1, -jnp.inf), axis=-1)
        p2 = jax.nn.softmax(jnp.where(allowed, s2, -jnp.inf), axis=-1)
        w = (p1 - lam * p2).astype(dtype)
        return jnp.einsum('bhqk,bhkd->bhqd', w, vh)

    o = lax.map(one_block, jnp.arange(n_blocks))
    o = jnp.transpose(o, (1, 0, 3, 2, 4)).reshape(bsz, seq, ATTN_HEADS, ATTN_V_DIM)
    o = rms_norm(o, subln_g) * (1.0 - lam_init)
    return o.reshape(bsz, seq, ATTN_WIDTH).astype(dtype)


def _complex_affine_combine(e1, e2):
    a1r, a1i, b1r, b1i = e1
    a2r, a2i, b2r, b2i = e2
    ar = a1r * a2r - a1i * a2i
    ai = a1r * a2i + a1i * a2r
    br = a2r * b1r - a2i * b1i + b2r
    bi = a2r * b1i + a2i * b1r + b2i
    return (ar, ai, br, bi)


def s5_branch(u, a_re, a_im, log_dt, b_re, b_im, c_re, c_im, d, w_glu, b_glu):
    dtype = u.dtype
    bsz, seq, _ = u.shape
    uf = u.astype(jnp.float32)
    ug = uf.reshape(bsz, seq, SSM_GROUPS, SSM_GROUP)
    ar = a_re.astype(jnp.float32)
    ai = a_im.astype(jnp.float32)
    dt = jnp.exp(log_dt.astype(jnp.float32))[:, None]
    mag = jnp.exp(ar * dt)
    ang = ai * dt
    lb_re = mag * jnp.cos(ang)
    lb_im = mag * jnp.sin(ang)
    den = ar * ar + ai * ai
    nr = lb_re - 1.0
    ni = lb_im
    f_re = (nr * ar + ni * ai) / den
    f_im = (ni * ar - nr * ai) / den
    br = b_re.astype(jnp.float32)
    bi = b_im.astype(jnp.float32)
    bb_re = f_re[..., None] * br - f_im[..., None] * bi
    bb_im = f_re[..., None] * bi + f_im[..., None] * br
    bu_re = jnp.einsum('blgn,gpn->blgp', ug, bb_re)
    bu_im = jnp.einsum('blgn,gpn->blgp', ug, bb_im)
    a_r = jnp.broadcast_to(lb_re, (1, seq, SSM_GROUPS, SSM_STATE))
    a_i = jnp.broadcast_to(lb_im, (1, seq, SSM_GROUPS, SSM_STATE))
    _, _, xr, xi = lax.associative_scan(_complex_affine_combine, (a_r, a_i, bu_re, bu_im), axis=1)
    y = (jnp.einsum('blgp,gnp->blgn', xr, c_re.astype(jnp.float32))
         - jnp.einsum('blgp,gnp->blgn', xi, c_im.astype(jnp.float32)))
    y = y.reshape(bsz, seq, SSM_WIDTH) + d.astype(jnp.float32) * uf
    z = jax.nn.gelu(y).astype(dtype)
    return z * jax.nn.sigmoid(z @ w_glu + b_glu)


def hybrid_layer(x, layer_idx, norm_mix_g, w_in, b_gate, q_norm_g, k_norm_g,
                 lambda_q1, lambda_k1, lambda_q2, lambda_k2, subln_g,
                 ssm_a_re, ssm_a_im, ssm_log_dt, ssm_b_re, ssm_b_im, ssm_c_re, ssm_c_im,
                 ssm_d, w_glu, b_glu, w_proj_attn, w_proj_ssm, w_out,
                 norm_mlp_g, w_mlp_in, w_mlp_out):
    h = rms_norm(x, norm_mix_g)
    proj = h @ w_in
    o1 = ATTN_QK_WIDTH
    o2 = o1 + ATTN_QK_WIDTH
    o3 = o2 + ATTN_WIDTH
    o4 = o3 + SSM_WIDTH
    q, k, v, u, g = proj[..., :o1], proj[..., o1:o2], proj[..., o2:o3], proj[..., o3:o4], proj[..., o4:]
    gates = jax.nn.sigmoid(g + b_gate)
    g_attn, g_ssm = gates[..., :D_MODEL], gates[..., D_MODEL:]

    a = diff_attention(q, k, v, q_norm_g, k_norm_g, lambda_q1, lambda_k1, lambda_q2, lambda_k2,
                       subln_g, layer_idx)
    s = s5_branch(u, ssm_a_re, ssm_a_im, ssm_log_dt, ssm_b_re, ssm_b_im, ssm_c_re, ssm_c_im,
                  ssm_d, w_glu, b_glu)
    merged = g_attn * (a @ w_proj_attn) + g_ssm * (s @ w_proj_ssm)
    x = x + merged @ w_out

    hm = rms_norm(x, norm_mlp_g)
    x = x + jnp.square(jax.nn.relu(hm @ w_mlp_in)) @ w_mlp_out
    return x


def setup_inputs(seed: int = 0) -> dict:
    key = jax.random.key(seed)
    ks = jax.random.split(key, 32)
    L = DEPTH
    f32 = jnp.float32

    def nrm(k, shape, scale):
        return jax.random.normal(k, shape, f32) * scale

    x = jax.random.normal(ks[0], (BATCH, SEQ, D_MODEL), f32)
    a_re = -0.5 + 0.01 * jax.random.normal(ks[1], (L, SSM_GROUPS, SSM_STATE), f32)
    a_im = (jnp.pi * jnp.arange(SSM_STATE, dtype=f32))[None, None, :] \
        + 0.01 * jax.random.normal(ks[2], (L, SSM_GROUPS, SSM_STATE), f32)
    log_dt = jax.random.uniform(ks[3], (L, SSM_GROUPS), f32, math.log(DT_MIN), math.log(DT_MAX))
    return {
        "x": x,
        "norm_mix_g": 1.0 + nrm(ks[4], (L, D_MODEL), 0.02),
        "w_in": nrm(ks[5], (L, D_MODEL, IN_COLS), D_MODEL ** -0.5),
        "b_gate": nrm(ks[6], (L, N_BRANCH * D_MODEL), 0.02),
        "q_norm_g": 1.0 + nrm(ks[7], (L, ATTN_QK_DIM), 0.02),
        "k_norm_g": 1.0 + nrm(ks[8], (L, ATTN_QK_DIM), 0.02),
        "lambda_q1": nrm(ks[9], (L, ATTN_QK_DIM), 0.1),
        "lambda_k1": nrm(ks[10], (L, ATTN_QK_DIM), 0.1),
        "lambda_q2": nrm(ks[11], (L, ATTN_QK_DIM), 0.1),
        "lambda_k2": nrm(ks[12], (L, ATTN_QK_DIM), 0.1),
        "subln_g": 1.0 + nrm(ks[13], (L, ATTN_V_DIM), 0.02),
        "ssm_a_re": a_re,
        "ssm_a_im": a_im,
        "ssm_log_dt": log_dt,
        "ssm_b_re": nrm(ks[14], (L, SSM_GROUPS, SSM_STATE, SSM_GROUP), (0.5 / SSM_GROUP) ** 0.5),
        "ssm_b_im": nrm(ks[15], (L, SSM_GROUPS, SSM_STATE, SSM_GROUP), (0.5 / SSM_GROUP) ** 0.5),
        "ssm_c_re": nrm(ks[16], (L, SSM_GROUPS, SSM_GROUP, SSM_STATE), (0.5 / SSM_STATE) ** 0.5),
        "ssm_c_im": nrm(ks[17], (L, SSM_GROUPS, SSM_GROUP, SSM_STATE), (0.5 / SSM_STATE) ** 0.5),
        "ssm_d": nrm(ks[18], (L, SSM_WIDTH), 1.0),
        "w_glu": nrm(ks[19], (L, SSM_WIDTH, SSM_WIDTH), SSM_WIDTH ** -0.5),
        "b_glu": nrm(ks[20], (L, SSM_WIDTH), 0.02),
        "w_proj_attn": nrm(ks[21], (L, ATTN_WIDTH, D_MODEL), ATTN_WIDTH ** -0.5),
        "w_proj_ssm": nrm(ks[22], (L, SSM_WIDTH, D_MODEL), SSM_WIDTH ** -0.5),
        "w_out": nrm(ks[23], (L, D_MODEL, D_MODEL), D_MODEL ** -0.5),
        "norm_mlp_g": 1.0 + nrm(ks[24], (L, D_MODEL), 0.02),
        "w_mlp_in": nrm(ks[25], (L, D_MODEL, D_FF), D_MODEL ** -0.5),
        "w_mlp_out": nrm(ks[26], (L, D_FF, D_MODEL), D_FF ** -0.5),
    }


def reference(x, norm_mix_g, w_in, b_gate, q_norm_g, k_norm_g, lambda_q1, lambda_k1,
              lambda_q2, lambda_k2, subln_g, ssm_a_re, ssm_a_im, ssm_log_dt, ssm_b_re,
              ssm_b_im, ssm_c_re, ssm_c_im, ssm_d, w_glu, b_glu, w_proj_attn, w_proj_ssm,
              w_out, norm_mlp_g, w_mlp_in, w_mlp_out):
    for l in range(DEPTH):
        x = hybrid_layer(
            x, l, norm_mix_g[l], w_in[l], b_gate[l], q_norm_g[l], k_norm_g[l],
            lambda_q1[l], lambda_k1[l], lambda_q2[l], lambda_k2[l], subln_g[l],
            ssm_a_re[l], ssm_a_im[l], ssm_log_dt[l], ssm_b_re[l], ssm_b_im[l],
            ssm_c_re[l], ssm_c_im[l], ssm_d[l], w_glu[l], b_glu[l],
            w_proj_attn[l], w_proj_ssm[l], w_out[l],
            norm_mlp_g[l], w_mlp_in[l], w_mlp_out[l])
    return x
```

```python
import functools
import math

import jax
import jax.numpy as jnp
from jax import lax
from jax.experimental import pallas as pl
from jax.experimental.pallas import tpu as pltpu

F32 = jnp.float32
BF16 = jnp.bfloat16

EPS = 1e-6
CHUNK = 64
ATTN_HEADS = 4
ATTN_QK_DIM = 64
ATTN_V_DIM = 2 * ATTN_QK_DIM
SSM_GROUP = 16
SSM_STATE = 64
SSM_T = 32
V7X_VMEM_LIMIT_BYTES = 56 * 1024 * 1024

TOKEN_TILE = 512
ATTN_TILE = 512
MLP_FF_TILE = 1024


def _const_spec(shape):
    zeros = (0,) * len(shape)
    return pl.BlockSpec(shape, lambda *_: zeros, pipeline_mode=pl.Buffered(1))


def _in_proj_kernel(x_ref, g_ref, w_ref, bd_ref, qg_ref, kg_ref, bg_ref,
                    q_ref, k_ref, v_ref, u_ref, gate_ref, *, qk_w, v_w, u_w):
    x = x_ref[...]
    ms = jnp.mean(x * x, axis=-1, keepdims=True)
    h = (x * lax.rsqrt(ms + EPS) * g_ref[...]).astype(BF16)

    def proj(lo, hi):
        return jnp.dot(h, w_ref[:, lo:hi], preferred_element_type=F32)

    def group_norm(t, gain):
        ms_g = jnp.dot((t * t).astype(BF16), bd_ref[...], preferred_element_type=F32)
        return t * lax.rsqrt(ms_g + EPS) * gain

    o1 = qk_w
    o2 = o1 + qk_w
    o3 = o2 + v_w
    o4 = o3 + u_w
    q_ref[...] = group_norm(proj(0, o1), qg_ref[...]).astype(BF16)
    k_ref[...] = group_norm(proj(o1, o2), kg_ref[...]).astype(BF16)
    v_ref[...] = proj(o2, o3).astype(BF16)
    u_ref[...] = proj(o3, o4).astype(BF16)
    gate_ref[...] = jax.nn.sigmoid(proj(o4, w_ref.shape[1]) + bg_ref[...]).astype(BF16)


def _in_proj(x2, g, w, bd, qg, kg, bg, *, qk_w, v_w, u_w, tm):
    n, d = x2.shape
    gate_w = w.shape[1] - 2 * qk_w - v_w - u_w
    row = lambda i: (i, 0)
    out_shapes = (
        jax.ShapeDtypeStruct((n, qk_w), BF16),
        jax.ShapeDtypeStruct((n, qk_w), BF16),
        jax.ShapeDtypeStruct((n, v_w), BF16),
        jax.ShapeDtypeStruct((n, u_w), BF16),
        jax.ShapeDtypeStruct((n, gate_w), BF16),
    )
    return pl.pallas_call(
        functools.partial(_in_proj_kernel, qk_w=qk_w, v_w=v_w, u_w=u_w),
        grid=(n // tm,),
        in_specs=[
            pl.BlockSpec((tm, d), row),
            _const_spec(g.shape), _const_spec(w.shape), _const_spec(bd.shape),
            _const_spec(qg.shape), _const_spec(kg.shape), _const_spec(bg.shape),
        ],
        out_specs=(
            pl.BlockSpec((tm, qk_w), row), pl.BlockSpec((tm, qk_w), row),
            pl.BlockSpec((tm, v_w), row), pl.BlockSpec((tm, u_w), row),
            pl.BlockSpec((tm, gate_w), row),
        ),
        out_shape=out_shapes,
        compiler_params=pltpu.CompilerParams(
            dimension_semantics=("parallel",), vmem_limit_bytes=V7X_VMEM_LIMIT_BYTES),
        name="in_proj",
    )(x2, g, w, bd, qg, kg, bg)


def _attn_kernel(lam_ref, q_ref, k_ref, vt_ref, sg_ref, o_ref, m_ref, l_ref, acc_ref, *, t):
    i = pl.program_id(2)
    q = q_ref[0]
    lane = lax.broadcasted_iota(jnp.int32, q.shape, 1)
    zero = jnp.zeros_like(q)
    qcat = jnp.concatenate([jnp.where(lane < ATTN_QK_DIM, q, zero),
                            jnp.where(lane >= ATTN_QK_DIM, q, zero)], axis=0)

    m_ref[...] = jnp.full(m_ref.shape, -jnp.inf, F32)
    l_ref[...] = jnp.zeros(l_ref.shape, F32)
    acc_ref[...] = jnp.zeros(acc_ref.shape, F32)

    def step(j, masked):
        kt = k_ref[0, pl.ds(pl.multiple_of(j * t, t), t), :]
        s = lax.dot_general(kt, qcat, (((1,), (1,)), ((), ())),
                            preferred_element_type=F32)
        if masked:
            key_chunk = lax.broadcasted_iota(jnp.int32, s.shape, 0) // CHUNK
            qry = lax.broadcasted_iota(jnp.int32, s.shape, 1)
            qry_chunk = jnp.where(qry >= t, qry - t, qry) // CHUNK
            s = jnp.where(key_chunk <= qry_chunk, s, -jnp.inf)
        m_old = m_ref[...]
        m_new = jnp.maximum(m_old, jnp.max(s, axis=0, keepdims=True))
        alpha = jnp.exp(m_old - m_new)
        p = jnp.exp(s - m_new)
        l_ref[...] = alpha * l_ref[...] + jnp.sum(p, axis=0, keepdims=True)
        acc_ref[...] = alpha * acc_ref[...] + jnp.dot(
            vt_ref[0, 0, j], p.astype(BF16), preferred_element_type=F32)
        m_ref[...] = m_new

    def body(j, carry):
        step(j, False)
        return carry

    lax.fori_loop(0, i, body, 0)
    step(i, True)

    o = acc_ref[...] * (1.0 / l_ref[...])
    o = o[:, :t] - lam_ref[0] * o[:, t:]
    ms = jnp.mean(o * o, axis=0, keepdims=True)
    o = o * lax.rsqrt(ms + EPS) * sg_ref[...]
    o_ref[0] = o.T.astype(BF16)


def _attention(lam, q, k, vt, sg, *, t):
    b, s, w = q.shape
    heads = w // ATTN_V_DIM
    nk = s // t
    return pl.pallas_call(
        functools.partial(_attn_kernel, t=t),
        grid=(b, heads, nk),
        in_specs=[
            pl.BlockSpec(memory_space=pltpu.SMEM),
            pl.BlockSpec((1, t, ATTN_V_DIM), lambda bi, hi, i: (bi, i, hi)),
            pl.BlockSpec((1, s, ATTN_V_DIM), lambda bi, hi, i: (bi, 0, hi)),
            pl.BlockSpec((1, 1, nk, ATTN_V_DIM, t), lambda bi, hi, i: (bi, hi, 0, 0, 0)),
            _const_spec(sg.shape),
        ],
        out_specs=pl.BlockSpec((1, t, ATTN_V_DIM), lambda bi, hi, i: (bi, i, hi)),
        out_shape=jax.ShapeDtypeStruct((b, s, w), BF16),
        scratch_shapes=[
            pltpu.VMEM((1, 2 * t), F32),
            pltpu.VMEM((1, 2 * t), F32),
            pltpu.VMEM((ATTN_V_DIM, 2 * t), F32),
        ],
        compiler_params=pltpu.CompilerParams(
            dimension_semantics=("parallel", "parallel", "arbitrary"),
            vmem_limit_bytes=V7X_VMEM_LIMIT_BYTES),
        name="diff_attn",
    )(lam, q, k, vt, sg)


def _ssm_kernel(u_ref, toep_ref, gm_ref, gms_ref, h_ref, lam_ref, y_ref,
                xc_ref, xcs_ref, x0_ref, *, nb, nc):
    u = u_ref[...]
    xc_ref[...] = jnp.dot(u, gm_ref[0], preferred_element_type=F32)
    xcs_ref[...] = jnp.dot(u, gms_ref[0], preferred_element_type=F32)

    lam = lam_ref[0]
    a, b, c = lam[0:1], lam[1:2], lam[2:3]

    def body(ci, carry):
        st, st_sw = carry
        r = pl.multiple_of(ci * nb, nb)
        x0_ref[pl.ds(r, nb), :] = st
        return (st * a + st_sw * b + xc_ref[pl.ds(r, nb), :],
                st_sw * a + st * c + xcs_ref[pl.ds(r, nb), :])

    z = jnp.zeros((nb, 2 * SSM_STATE), F32)
    lax.fori_loop(0, nc, body, (z, z))

    x0 = x0_ref[...]
    x0_hi = x0.astype(BF16)
    x0_lo = (x0 - x0_hi.astype(F32)).astype(BF16)
    y = jnp.dot(u, toep_ref[0], preferred_element_type=F32)
    y = y + jnp.dot(x0_hi, h_ref[0], preferred_element_type=F32)
    y = y + jnp.dot(x0_lo, h_ref[0], preferred_element_type=F32)
    y_ref[...] = y.astype(BF16)


def _ssm(ut, toep, gm, gms, hm, lam_t, *, nb, nc):
    rows, width = ut.shape
    groups = toep.shape[0]
    gw = width // groups
    st2 = 2 * SSM_STATE
    grp = lambda g: (g, 0, 0)
    return pl.pallas_call(
        functools.partial(_ssm_kernel, nb=nb, nc=nc),
        grid=(groups,),
        in_specs=[
            pl.BlockSpec((rows, gw), lambda g: (0, g)),
            pl.BlockSpec((1, gw, gw), grp),
            pl.BlockSpec((1, gw, st2), grp),
            pl.BlockSpec((1, gw, st2), grp),
            pl.BlockSpec((1, st2, gw), grp),
            pl.BlockSpec((1, 8, st2), grp),
        ],
        out_specs=pl.BlockSpec((rows, gw), lambda g: (0, g)),
        out_shape=jax.ShapeDtypeStruct((rows, width), BF16),
        scratch_shapes=[pltpu.VMEM((rows, st2), F32)] * 3,
        compiler_params=pltpu.CompilerParams(
            dimension_semantics=("parallel",), vmem_limit_bytes=V7X_VMEM_LIMIT_BYTES),
        name="s5_ssm",
    )(ut, toep, gm, gms, hm, lam_t)


def _ssm_operators(a_re, a_im, log_dt, b_re, b_im, c_re, c_im, d, t):
    hp = lax.Precision.HIGHEST
    dt = jnp.exp(log_dt)[:, None]
    lre = a_re * dt
    ang = a_im * dt
    mag = jnp.exp(lre)
    lb_re, lb_im = mag * jnp.cos(ang), mag * jnp.sin(ang)
    den = a_re * a_re + a_im * a_im
    nr, ni = lb_re - 1.0, lb_im
    f_re = (nr * a_re + ni * a_im) / den
    f_im = (ni * a_re - nr * a_im) / den
    bb_re = f_re[..., None] * b_re - f_im[..., None] * b_im
    bb_im = f_re[..., None] * b_im + f_im[..., None] * b_re
    k = jnp.arange(t + 1, dtype=F32)[:, None, None]
    pmag = jnp.exp(k * lre)
    p_re, p_im = pmag * jnp.cos(k * ang), pmag * jnp.sin(k * ang)
    cl_re = c_re[None] * p_re[:, :, None, :] - c_im[None] * p_im[:, :, None, :]
    cl_im = c_re[None] * p_im[:, :, None, :] + c_im[None] * p_re[:, :, None, :]
    kmat = (jnp.einsum('lgnp,gpm->glnm', cl_re[:t], bb_re, precision=hp)
            - jnp.einsum('lgnp,gpm->glnm', cl_im[:t], bb_im, precision=hp))
    g = a_re.shape[0]
    eye = jnp.eye(SSM_GROUP, dtype=F32)
    kmat = kmat.at[:, 0].add(d.reshape(g, SSM_GROUP)[:, :, None] * eye[None])
    kv = jnp.transpose(kmat, (0, 3, 2, 1))
    kv = jnp.concatenate([kv, jnp.zeros_like(kv)], axis=-1)
    kv = jnp.broadcast_to(kv[..., None, :], kv.shape[:-1] + (t, 2 * t))
    kv = kv.reshape(kv.shape[:-2] + (2 * t * t,))[..., :t * (2 * t - 1)]
    kv = kv.reshape(kv.shape[:-1] + (t, 2 * t - 1))[..., :t]
    toep = jnp.transpose(kv, (0, 1, 3, 2, 4)).reshape(g, SSM_GROUP * t, SSM_GROUP * t)
    h_re = jnp.transpose(cl_re[1:], (1, 3, 2, 0))
    h_im = jnp.transpose(cl_im[1:], (1, 3, 2, 0))
    hm = jnp.concatenate([h_re, -h_im], axis=1).reshape(g, 2 * SSM_STATE, SSM_GROUP * t)
    q_re = jnp.transpose(p_re[t - 1::-1][:t], (1, 0, 2))
    q_im = jnp.transpose(p_im[t - 1::-1][:t], (1, 0, 2))
    bt_re = jnp.transpose(bb_re, (0, 2, 1))[:, :, None, :]
    bt_im = jnp.transpose(bb_im, (0, 2, 1))[:, :, None, :]
    g_re = bt_re * q_re[:, None] - bt_im * q_im[:, None]
    g_im = bt_re * q_im[:, None] + bt_im * q_re[:, None]
    gm = jnp.concatenate([g_re, g_im], axis=-1).reshape(g, SSM_GROUP * t, 2 * SSM_STATE)
    gms = jnp.concatenate([g_im, g_re], axis=-1).reshape(g, SSM_GROUP * t, 2 * SSM_STATE)
    tr, ti = p_re[t], p_im[t]
    lam_t = jnp.stack([jnp.concatenate([tr, tr], -1), jnp.concatenate([-ti, ti], -1),
                       jnp.concatenate([ti, -ti], -1)], axis=1)
    lam_t = jnp.concatenate([lam_t, jnp.zeros((g, 5, 2 * SSM_STATE), F32)], axis=1)
    return toep.astype(BF16), gm.astype(BF16), gms.astype(BF16), hm.astype(BF16), lam_t


def _merge_mlp_kernel(x_ref, a_ref, y_ref, gate_ref, wglu_ref, bglu_ref, wpa_ref, wps_ref,
                      wout_ref, gm_ref, wmi_ref, wmo_ref, o_ref, *, ff_tile):
    d = x_ref.shape[1]
    z = jax.nn.gelu(y_ref[...].astype(F32), approximate=True)
    glu = jnp.dot(z.astype(BF16), wglu_ref[...], preferred_element_type=F32) + bglu_ref[...]
    s = z * jax.nn.sigmoid(glu)
    pa = jnp.dot(a_ref[...], wpa_ref[...], preferred_element_type=F32)
    ps = jnp.dot(s.astype(BF16), wps_ref[...], preferred_element_type=F32)
    merged = gate_ref[:, :d].astype(F32) * pa + gate_ref[:, d:].astype(F32) * ps
    x1 = x_ref[...] + jnp.dot(merged.astype(BF16), wout_ref[...], preferred_element_type=F32)
    ms = jnp.mean(x1 * x1, axis=-1, keepdims=True)
    hm = (x1 * lax.rsqrt(ms + EPS) * gm_ref[...]).astype(BF16)
    acc = x1
    for j in range(wmi_ref.shape[1] // ff_tile):
        hj = jnp.dot(hm, wmi_ref[:, j * ff_tile:(j + 1) * ff_tile], preferred_element_type=F32)
        hj = jnp.square(jnp.maximum(hj, 0.0)).astype(BF16)
        acc = acc + jnp.dot(hj, wmo_ref[j * ff_tile:(j + 1) * ff_tile, :],
                            preferred_element_type=F32)
    o_ref[...] = acc


def _merge_mlp(x2, a, y, gates, wglu, bglu, wpa, wps, wout, gmlp, wmi, wmo, *, tm, ff_tile):
    n, d = x2.shape
    row = lambda i: (i, 0)
    consts = (wglu, bglu, wpa, wps, wout, gmlp, wmi, wmo)
    return pl.pallas_call(
        functools.partial(_merge_mlp_kernel, ff_tile=ff_tile),
        grid=(n // tm,),
        in_specs=[
            pl.BlockSpec((tm, d), row), pl.BlockSpec((tm, a.shape[1]), row),
            pl.BlockSpec((tm, y.shape[1]), row), pl.BlockSpec((tm, gates.shape[1]), row),
        ] + [_const_spec(c.shape) for c in consts],
        out_specs=pl.BlockSpec((tm, d), row),
        out_shape=jax.ShapeDtypeStruct((n, d), F32),
        compiler_params=pltpu.CompilerParams(
            dimension_semantics=("parallel",), vmem_limit_bytes=V7X_VMEM_LIMIT_BYTES),
        name="merge_mlp",
    )(x2, a, y, gates, *consts)


def _lambda_init(layer_idx):
    return 0.8 - 0.6 * math.exp(-0.3 * layer_idx)


def _layer(x, layer_idx, norm_mix_g, w_in, b_gate, q_norm_g, k_norm_g, lambda_q1, lambda_k1,
           lambda_q2, lambda_k2, subln_g, ssm_a_re, ssm_a_im, ssm_log_dt, ssm_b_re, ssm_b_im,
           ssm_c_re, ssm_c_im, ssm_d, w_glu, b_glu, w_proj_attn, w_proj_ssm, w_out,
           norm_mlp_g, w_mlp_in, w_mlp_out):
    bsz, seq, d = x.shape
    n = bsz * seq
    qk_w = ATTN_HEADS * 2 * ATTN_QK_DIM
    v_w = ATTN_HEADS * ATTN_V_DIM
    u_w = ssm_d.shape[0]
    x2 = x.reshape(n, d)
    row = lambda v: v.reshape(1, -1).astype(F32)

    group_mean = jnp.kron(jnp.eye(qk_w // ATTN_QK_DIM, dtype=F32),
                          jnp.full((ATTN_QK_DIM, ATTN_QK_DIM), 1.0 / ATTN_QK_DIM, F32)).astype(BF16)
    reps = qk_w // ATTN_QK_DIM
    q_gain = row(jnp.tile(q_norm_g.astype(F32) * ATTN_QK_DIM ** -0.5, reps))
    k_gain = row(jnp.tile(k_norm_g.astype(F32), reps))
    q, k, v, u, gates = _in_proj(
        x2, row(norm_mix_g), w_in.astype(BF16), group_mean, q_gain, k_gain, row(b_gate),
        qk_w=qk_w, v_w=v_w, u_w=u_w, tm=min(TOKEN_TILE, n))

    t = min(ATTN_TILE, seq)
    lam_init = _lambda_init(layer_idx)
    lam = (jnp.exp(jnp.sum(lambda_q1.astype(F32) * lambda_k1.astype(F32)))
           - jnp.exp(jnp.sum(lambda_q2.astype(F32) * lambda_k2.astype(F32))) + lam_init)
    vt = jnp.transpose(v.reshape(bsz, seq // t, t, ATTN_HEADS, ATTN_V_DIM), (0, 3, 1, 4, 2))
    sub_gain = (subln_g.astype(F32) * (1.0 - lam_init)).reshape(-1, 1)
    a = _attention(lam.reshape(1), q.reshape(bsz, seq, qk_w), k.reshape(bsz, seq, qk_w), vt,
                   sub_gain, t=t)

    tt = SSM_T
    nc = seq // tt
    toep, gm, gms, hm, lam_t = _ssm_operators(
        ssm_a_re.astype(F32), ssm_a_im.astype(F32), ssm_log_dt.astype(F32),
        ssm_b_re.astype(F32), ssm_b_im.astype(F32), ssm_c_re.astype(F32), ssm_c_im.astype(F32),
        ssm_d.astype(F32), tt)
    ut = jnp.transpose(u.reshape(bsz, nc, tt, u_w), (1, 0, 3, 2)).reshape(nc * bsz, u_w * tt)
    yt = _ssm(ut, toep, gm, gms, hm, lam_t, nb=bsz, nc=nc)
    y = jnp.transpose(yt.reshape(nc, bsz, u_w, tt), (1, 0, 3, 2)).reshape(n, u_w)

    out = _merge_mlp(
        x2, a.reshape(n, v_w), y, gates, w_glu.astype(BF16), row(b_glu),
        w_proj_attn.astype(BF16), w_proj_ssm.astype(BF16), w_out.astype(BF16),
        row(norm_mlp_g), w_mlp_in.astype(BF16), w_mlp_out.astype(BF16),
        tm=min(TOKEN_TILE, n), ff_tile=MLP_FF_TILE)
    return out.reshape(bsz, seq, d)


def kernel(x, norm_mix_g, w_in, b_gate, q_norm_g, k_norm_g, lambda_q1, lambda_k1, lambda_q2, lambda_k2, subln_g, ssm_a_re, ssm_a_im, ssm_log_dt, ssm_b_re, ssm_b_im, ssm_c_re, ssm_c_im, ssm_d, w_glu, b_glu, w_proj_attn, w_proj_ssm, w_out, norm_mlp_g, w_mlp_in, w_mlp_out):
    for l in range(norm_mix_g.shape[0]):
        x = _layer(
            x, l, norm_mix_g[l], w_in[l], b_gate[l], q_norm_g[l], k_norm_g[l],
            lambda_q1[l], lambda_k1[l], lambda_q2[l], lambda_k2[l], subln_g[l],
            ssm_a_re[l], ssm_a_im[l], ssm_log_dt[l], ssm_b_re[l], ssm_b_im[l],
            ssm_c_re[l], ssm_c_im[l], ssm_d[l], w_glu[l], b_glu[l],
            w_proj_attn[l], w_proj_ssm[l], w_out[l],
            norm_mlp_g[l], w_mlp_in[l], w_mlp_out[l])
    return x
```

```python
import functools
import math

import jax
import jax.numpy as jnp
from jax import lax
from jax.experimental import pallas as pl
from jax.experimental.pallas import tpu as pltpu

F32 = jnp.float32
BF16 = jnp.bfloat16

EPS = 1e-6
CHUNK = 64
ATTN_HEADS = 4
ATTN_QK_DIM = 64
ATTN_V_DIM = 2 * ATTN_QK_DIM
SSM_GROUP = 16
SSM_STATE = 64
LANES = 128
SSM_T = LANES
V7X_VMEM_LIMIT_BYTES = 56 * 1024 * 1024

TOKEN_TILE = 512
ATTN_TILE = 512
MLP_FF_TILE = 1024


def _const_spec(shape):
    zeros = (0,) * len(shape)
    return pl.BlockSpec(shape, lambda *_: zeros, pipeline_mode=pl.Buffered(1))


def _in_proj_kernel(x_ref, g_ref, w_ref, bd_ref, qg_ref, kg_ref, bg_ref,
                    q_ref, k_ref, vt_ref, ut_ref, gate_ref, vu_ref, *, qk_w, v_w, u_w):
    x = x_ref[0]
    ms = jnp.mean(x * x, axis=-1, keepdims=True)
    h = (x * lax.rsqrt(ms + EPS) * g_ref[...]).astype(BF16)

    def proj(lo, hi):
        return jnp.dot(h, w_ref[:, lo:hi], preferred_element_type=F32)

    def group_norm(t, gain):
        ms_g = jnp.dot((t * t).astype(BF16), bd_ref[...], preferred_element_type=F32)
        return t * lax.rsqrt(ms_g + EPS) * gain

    o1 = qk_w
    o2 = o1 + qk_w
    o3 = o2 + v_w
    o4 = o3 + u_w
    q_ref[0] = group_norm(proj(0, o1), qg_ref[...]).astype(BF16)
    k_ref[0] = group_norm(proj(o1, o2), kg_ref[...]).astype(BF16)
    vu_ref[...] = proj(o2, o4)
    for hd in range(v_w // ATTN_V_DIM):
        vt_ref[0, hd, 0] = vu_ref[:, hd * ATTN_V_DIM:(hd + 1) * ATTN_V_DIM].T.astype(BF16)
    for j in range(ut_ref.shape[0]):
        ut_ref[j] = vu_ref[j * SSM_T:(j + 1) * SSM_T, v_w:].T
    gate_ref[0] = jax.nn.sigmoid(proj(o4, w_ref.shape[1]) + bg_ref[...]).astype(BF16)


def _in_proj(x, g, w, bd, qg, kg, bg, *, qk_w, v_w, u_w, tm):
    bsz, seq, d = x.shape
    gate_w = w.shape[1] - 2 * qk_w - v_w - u_w
    heads = v_w // ATTN_V_DIM
    tok = lambda b, i: (b, i, 0)
    out_shapes = (
        jax.ShapeDtypeStruct((bsz, seq, qk_w), BF16),
        jax.ShapeDtypeStruct((bsz, seq, qk_w), BF16),
        jax.ShapeDtypeStruct((bsz, heads, seq // tm, ATTN_V_DIM, tm), BF16),
        jax.ShapeDtypeStruct((seq // SSM_T, bsz, u_w, SSM_T), F32),
        jax.ShapeDtypeStruct((bsz, seq, gate_w), BF16),
    )
    return pl.pallas_call(
        functools.partial(_in_proj_kernel, qk_w=qk_w, v_w=v_w, u_w=u_w),
        grid=(bsz, seq // tm),
        in_specs=[
            pl.BlockSpec((1, tm, d), tok),
            _const_spec(g.shape), _const_spec(w.shape), _const_spec(bd.shape),
            _const_spec(qg.shape), _const_spec(kg.shape), _const_spec(bg.shape),
        ],
        out_specs=(
            pl.BlockSpec((1, tm, qk_w), tok), pl.BlockSpec((1, tm, qk_w), tok),
            pl.BlockSpec((1, heads, 1, ATTN_V_DIM, tm), lambda b, i: (b, 0, i, 0, 0)),
            pl.BlockSpec((tm // SSM_T, None, u_w, SSM_T), lambda b, i: (i, b, 0, 0)),
            pl.BlockSpec((1, tm, gate_w), tok),
        ),
        out_shape=out_shapes,
        scratch_shapes=[pltpu.VMEM((tm, v_w + u_w), F32)],
        compiler_params=pltpu.CompilerParams(
            dimension_semantics=("parallel", "parallel"), vmem_limit_bytes=V7X_VMEM_LIMIT_BYTES),
        name="in_proj",
    )(x, g, w, bd, qg, kg, bg)


def _attn_kernel(lam_ref, q_ref, k_ref, vt_ref, sg_ref, o_ref, m_ref, l_ref, acc_ref, *, t):
    i = pl.program_id(2)
    q = q_ref[0]
    lane = lax.broadcasted_iota(jnp.int32, q.shape, 1)
    zero = jnp.zeros_like(q)
    qcat = jnp.concatenate([jnp.where(lane < ATTN_QK_DIM, q, zero),
                            jnp.where(lane >= ATTN_QK_DIM, q, zero)], axis=0)

    m_ref[...] = jnp.full(m_ref.shape, -jnp.inf, F32)
    l_ref[...] = jnp.zeros(l_ref.shape, F32)
    acc_ref[...] = jnp.zeros(acc_ref.shape, F32)

    def step(j, masked):
        kt = k_ref[0, pl.ds(pl.multiple_of(j * t, t), t), :]
        s = lax.dot_general(kt, qcat, (((1,), (1,)), ((), ())),
                            preferred_element_type=F32)
        if masked:
            key_chunk = lax.broadcasted_iota(jnp.int32, s.shape, 0) // CHUNK
            qry = lax.broadcasted_iota(jnp.int32, s.shape, 1)
            qry_chunk = jnp.where(qry >= t, qry - t, qry) // CHUNK
            s = jnp.where(key_chunk <= qry_chunk, s, -jnp.inf)
        m_old = m_ref[...]
        m_new = jnp.maximum(m_old, jnp.max(s, axis=0, keepdims=True))
        alpha = jnp.exp(m_old - m_new)
        p = jnp.exp(s - m_new)
        l_ref[...] = alpha * l_ref[...] + jnp.sum(p, axis=0, keepdims=True)
        acc_ref[...] = alpha * acc_ref[...] + jnp.dot(
            vt_ref[0, 0, j], p.astype(BF16), preferred_element_type=F32)
        m_ref[...] = m_new

    def body(j, carry):
        step(j, False)
        return carry

    lax.fori_loop(0, i, body, 0)
    step(i, True)

    o = acc_ref[...] * (1.0 / l_ref[...])
    o = o[:, :t] - lam_ref[0] * o[:, t:]
    ms = jnp.mean(o * o, axis=0, keepdims=True)
    o = o * lax.rsqrt(ms + EPS) * sg_ref[...]
    o_ref[0] = o.T.astype(BF16)


def _attention(lam, q, k, vt, sg, *, t):
    b, s, w = q.shape
    heads = w // ATTN_V_DIM
    nk = s // t
    return pl.pallas_call(
        functools.partial(_attn_kernel, t=t),
        grid=(b, heads, nk),
        in_specs=[
            pl.BlockSpec(memory_space=pltpu.SMEM),
            pl.BlockSpec((1, t, ATTN_V_DIM), lambda bi, hi, i: (bi, i, hi)),
            pl.BlockSpec((1, s, ATTN_V_DIM), lambda bi, hi, i: (bi, 0, hi)),
            pl.BlockSpec((1, 1, nk, ATTN_V_DIM, t), lambda bi, hi, i: (bi, hi, 0, 0, 0)),
            _const_spec(sg.shape),
        ],
        out_specs=pl.BlockSpec((1, t, ATTN_V_DIM), lambda bi, hi, i: (bi, i, hi)),
        out_shape=jax.ShapeDtypeStruct((b, s, w), BF16),
        scratch_shapes=[
            pltpu.VMEM((1, 2 * t), F32),
            pltpu.VMEM((1, 2 * t), F32),
            pltpu.VMEM((ATTN_V_DIM, 2 * t), F32),
        ],
        compiler_params=pltpu.CompilerParams(
            dimension_semantics=("parallel", "parallel", "arbitrary"),
            vmem_limit_bytes=V7X_VMEM_LIMIT_BYTES),
        name="diff_attn",
    )(lam, q, k, vt, sg)


def _ssm_kernel(*refs, nb, nc):
    ng = SSM_GROUP
    (u_ref, ca_ref, cb_ref, ba_ref, bb_ref, bas_ref, bbs_ref, b2_ref, d_ref,
     pa0_ref, pb0_ref, pa1_ref, pb1_ref, qa_ref, qb_ref, lam_ref,
     y_ref, toep_ref, kr_ref, xc_ref, xcs_ref, x0_ref) = refs
    t = SSM_T
    gw = ng * t

    def rep(ref):
        v = ref[0]
        return jnp.broadcast_to(v[:, None, :], (ng, t, v.shape[-1])).reshape(gw, v.shape[-1])

    def tile(ref):
        v = ref[0]
        return jnp.broadcast_to(v[None], (ng, t, v.shape[-1])).reshape(gw, v.shape[-1])

    rca, rcb = rep(ca_ref), rep(cb_ref)
    cl0 = rca * tile(pa0_ref) + rcb * tile(pb0_ref)
    hm = (rca * tile(pa1_ref) + rcb * tile(pb1_ref)).astype(BF16)
    tqa, tqb = tile(qa_ref), tile(qb_ref)
    gm = (rep(ba_ref) * tqa + rep(bb_ref) * tqb).astype(BF16)
    gms = (rep(bas_ref) * tqa + rep(bbs_ref) * tqb).astype(BF16)

    kr = lax.dot_general(b2_ref[0], cl0, (((1,), (1,)), ((), ())),
                         precision=lax.Precision.HIGHEST, preferred_element_type=F32)
    lane = lax.broadcasted_iota(jnp.int32, kr.shape, 1)
    rowi = lax.broadcasted_iota(jnp.int32, kr.shape, 0)
    dfull = jnp.concatenate([d_ref[0]] * ng, axis=1)
    kr_ref[...] = kr + jnp.where(lane == rowi * t, dfull, 0.0)

    causal = (lax.broadcasted_iota(jnp.int32, (t, t), 1)
              >= lax.broadcasted_iota(jnp.int32, (t, t), 0))

    def fill(n_in, carry):
        row = kr_ref[pl.ds(n_in, 1), :]
        r0 = pl.multiple_of(n_in * t, t)
        for n_out in range(ng):
            seg = jnp.broadcast_to(row[:, n_out * t:(n_out + 1) * t], (t, t))
            blk = pltpu.roll(seg, 0, 1, stride=1, stride_axis=0)
            toep_ref[pl.ds(r0, t), n_out * t:(n_out + 1) * t] = (
                jnp.where(causal, blk, 0.0).astype(BF16))
        return carry

    lax.fori_loop(0, ng, fill, 0)

    u = jnp.concatenate([u_ref[:, n, :] for n in range(ng)], axis=1).astype(BF16)
    xc_ref[...] = jnp.dot(u, gm, preferred_element_type=F32)
    xcs_ref[...] = jnp.dot(u, gms, preferred_element_type=F32)

    lam = lam_ref[0]
    a, b, c = lam[0:1], lam[1:2], lam[2:3]

    def body(ci, carry):
        st, st_sw = carry
        r = pl.multiple_of(ci * nb, nb)
        x0_ref[pl.ds(r, nb), :] = st
        return (st * a + st_sw * b + xc_ref[pl.ds(r, nb), :],
                st_sw * a + st * c + xcs_ref[pl.ds(r, nb), :])

    z = jnp.zeros((nb, 2 * SSM_STATE), F32)
    lax.fori_loop(0, nc, body, (z, z))

    x0 = x0_ref[...]
    x0_hi = x0.astype(BF16)
    x0_lo = (x0 - x0_hi.astype(F32)).astype(BF16)
    nt = (((1,), (1,)), ((), ()))
    y = jnp.dot(u, toep_ref[...], preferred_element_type=F32)
    y = y + lax.dot_general(x0_hi, hm, nt, preferred_element_type=F32)
    y = y + lax.dot_general(x0_lo, hm, nt, preferred_element_type=F32)
    for n_out in range(ng):
        y_ref[:, n_out, :] = y[:, n_out * t:(n_out + 1) * t]


def _ssm(ut, tables, *, nb, nc):
    rows, width, t = ut.shape
    ng = SSM_GROUP
    groups = width // ng
    st2 = 2 * SSM_STATE
    gw = ng * t
    grp_spec = pl.BlockSpec((rows, ng, t), lambda g: (0, g, 0))
    tab_specs = [pl.BlockSpec((1,) + tab.shape[1:], lambda g: (g, 0, 0)) for tab in tables]
    return pl.pallas_call(
        functools.partial(_ssm_kernel, nb=nb, nc=nc),
        grid=(groups,),
        in_specs=[grp_spec] + tab_specs,
        out_specs=grp_spec,
        out_shape=jax.ShapeDtypeStruct((rows, width, t), F32),
        scratch_shapes=[
            pltpu.VMEM((gw, gw), BF16),
            pltpu.VMEM((ng, gw), F32),
            pltpu.VMEM((rows, st2), F32),
            pltpu.VMEM((rows, st2), F32),
            pltpu.VMEM((rows, st2), F32),
        ],
        compiler_params=pltpu.CompilerParams(
            dimension_semantics=("parallel",), vmem_limit_bytes=V7X_VMEM_LIMIT_BYTES),
        name="s5_ssm",
    )(ut, *tables)


def _ssm_tables(a_re, a_im, log_dt, b_re, b_im, c_re, c_im, d, t):
    dt = jnp.exp(log_dt)[:, None]
    lre = a_re * dt
    ang = a_im * dt
    mag = jnp.exp(lre)
    lb_re, lb_im = mag * jnp.cos(ang), mag * jnp.sin(ang)
    den = a_re * a_re + a_im * a_im
    nr, ni = lb_re - 1.0, lb_im
    f_re = (nr * a_re + ni * a_im) / den
    f_im = (ni * a_re - nr * a_im) / den
    bb_re = f_re[..., None] * b_re - f_im[..., None] * b_im
    bb_im = f_re[..., None] * b_im + f_im[..., None] * b_re
    bt_re, bt_im = jnp.swapaxes(bb_re, 1, 2), jnp.swapaxes(bb_im, 1, 2)
    cat = lambda x, y: jnp.concatenate([x, y], axis=-1)
    k = jnp.arange(t + 1, dtype=F32)[None, :, None]
    pmag = jnp.exp(k * lre[:, None, :])
    p_re, p_im = pmag * jnp.cos(k * ang[:, None, :]), pmag * jnp.sin(k * ang[:, None, :])
    pa, pb = cat(p_re, p_re), cat(p_im, p_im)
    tr, ti = p_re[:, t], p_im[:, t]
    lam_t = jnp.stack([cat(tr, tr), cat(-ti, ti), cat(ti, -ti)], axis=1)
    lam_t = jnp.concatenate([lam_t, jnp.zeros((lam_t.shape[0], 5, lam_t.shape[2]), F32)], axis=1)
    g = a_re.shape[0]
    d_tab = jnp.broadcast_to(d.reshape(g, SSM_GROUP, 1), (g, SSM_GROUP, t))
    return (
        cat(c_re, -c_im), cat(-c_im, -c_re),
        cat(bt_re, bt_im), cat(-bt_im, bt_re),
        cat(bt_im, bt_re), cat(bt_re, -bt_im),
        cat(bt_re, bt_im),
        d_tab,
        pa[:, :t], pb[:, :t],
        pa[:, 1:], pb[:, 1:],
        pa[:, t - 1::-1][:, :t], pb[:, t - 1::-1][:, :t],
        lam_t,
    )


def _merge_mlp_kernel(x_ref, a_ref, yt_ref, gate_ref, wglu_ref, bglu_ref, wpa_ref, wps_ref,
                      wout_ref, gm_ref, wmi_ref, wmo_ref, o_ref, *, ff_tile):
    d = x_ref.shape[2]
    y = jnp.concatenate([yt_ref[j].T for j in range(yt_ref.shape[0])], axis=0)
    z = jax.nn.gelu(y, approximate=True)
    glu = jnp.dot(z.astype(BF16), wglu_ref[...], preferred_element_type=F32) + bglu_ref[...]
    s = z * jax.nn.sigmoid(glu)
    pa = jnp.dot(a_ref[0], wpa_ref[...], preferred_element_type=F32)
    ps = jnp.dot(s.astype(BF16), wps_ref[...], preferred_element_type=F32)
    merged = gate_ref[0, :, :d].astype(F32) * pa + gate_ref[0, :, d:].astype(F32) * ps
    x1 = x_ref[0] + jnp.dot(merged.astype(BF16), wout_ref[...], preferred_element_type=F32)
    ms = jnp.mean(x1 * x1, axis=-1, keepdims=True)
    hm = (x1 * lax.rsqrt(ms + EPS) * gm_ref[...]).astype(BF16)
    acc = x1
    for j in range(wmi_ref.shape[1] // ff_tile):
        hj = jnp.dot(hm, wmi_ref[:, j * ff_tile:(j + 1) * ff_tile], preferred_element_type=F32)
        hj = jnp.square(jnp.maximum(hj, 0.0)).astype(BF16)
        acc = acc + jnp.dot(hj, wmo_ref[j * ff_tile:(j + 1) * ff_tile, :],
                            preferred_element_type=F32)
    o_ref[0] = acc


def _merge_mlp(x, a, yt, gates, wglu, bglu, wpa, wps, wout, gmlp, wmi, wmo, *, tm, ff_tile):
    bsz, seq, d = x.shape
    tok = lambda b, i: (b, i, 0)
    consts = (wglu, bglu, wpa, wps, wout, gmlp, wmi, wmo)
    return pl.pallas_call(
        functools.partial(_merge_mlp_kernel, ff_tile=ff_tile),
        grid=(bsz, seq // tm),
        in_specs=[
            pl.BlockSpec((1, tm, d), tok), pl.BlockSpec((1, tm, a.shape[2]), tok),
            pl.BlockSpec((tm // SSM_T, None, yt.shape[2], SSM_T), lambda b, i: (i, b, 0, 0)),
            pl.BlockSpec((1, tm, gates.shape[2]), tok),
        ] + [_const_spec(c.shape) for c in consts],
        out_specs=pl.BlockSpec((1, tm, d), tok),
        out_shape=jax.ShapeDtypeStruct((bsz, seq, d), F32),
        compiler_params=pltpu.CompilerParams(
            dimension_semantics=("parallel", "parallel"), vmem_limit_bytes=V7X_VMEM_LIMIT_BYTES),
        name="merge_mlp",
    )(x, a, yt, gates, *consts)


def _lambda_init(layer_idx):
    return 0.8 - 0.6 * math.exp(-0.3 * layer_idx)


def _layer(x, layer_idx, norm_mix_g, w_in, b_gate, q_norm_g, k_norm_g, lambda_q1, lambda_k1,
           lambda_q2, lambda_k2, subln_g, ssm_a_re, ssm_a_im, ssm_log_dt, ssm_b_re, ssm_b_im,
           ssm_c_re, ssm_c_im, ssm_d, w_glu, b_glu, w_proj_attn, w_proj_ssm, w_out,
           norm_mlp_g, w_mlp_in, w_mlp_out):
    bsz, seq, d = x.shape
    qk_w = ATTN_HEADS * 2 * ATTN_QK_DIM
    v_w = ATTN_HEADS * ATTN_V_DIM
    u_w = ssm_d.shape[0]
    row = lambda v: v.reshape(1, -1).astype(F32)
    tm = min(TOKEN_TILE, seq)
    t = min(ATTN_TILE, seq)
    assert tm == t and seq % tm == 0 and tm % SSM_T == 0

    group_mean = jnp.kron(jnp.eye(qk_w // ATTN_QK_DIM, dtype=F32),
                          jnp.full((ATTN_QK_DIM, ATTN_QK_DIM), 1.0 / ATTN_QK_DIM, F32)).astype(BF16)
    reps = qk_w // ATTN_QK_DIM
    q_gain = row(jnp.tile(q_norm_g.astype(F32) * ATTN_QK_DIM ** -0.5, reps))
    k_gain = row(jnp.tile(k_norm_g.astype(F32), reps))
    q, k, vt, ut, gates = _in_proj(
        x, row(norm_mix_g), w_in.astype(BF16), group_mean, q_gain, k_gain, row(b_gate),
        qk_w=qk_w, v_w=v_w, u_w=u_w, tm=tm)

    lam_init = _lambda_init(layer_idx)
    lam = (jnp.exp(jnp.sum(lambda_q1.astype(F32) * lambda_k1.astype(F32)))
           - jnp.exp(jnp.sum(lambda_q2.astype(F32) * lambda_k2.astype(F32))) + lam_init)
    sub_gain = (subln_g.astype(F32) * (1.0 - lam_init)).reshape(-1, 1)
    a = _attention(lam.reshape(1), q, k, vt, sub_gain, t=t)

    nc = seq // SSM_T
    tables = _ssm_tables(
        ssm_a_re.astype(F32), ssm_a_im.astype(F32), ssm_log_dt.astype(F32),
        ssm_b_re.astype(F32), ssm_b_im.astype(F32), ssm_c_re.astype(F32), ssm_c_im.astype(F32),
        ssm_d.astype(F32), SSM_T)
    yt = _ssm(ut.reshape(nc * bsz, u_w, SSM_T), tables, nb=bsz, nc=nc)
    yt = yt.reshape(nc, bsz, u_w, SSM_T)

    return _merge_mlp(
        x, a, yt, gates, w_glu.astype(BF16), row(b_glu),
        w_proj_attn.astype(BF16), w_proj_ssm.astype(BF16), w_out.astype(BF16),
        row(norm_mlp_g), w_mlp_in.astype(BF16), w_mlp_out.astype(BF16),
        tm=tm, ff_tile=MLP_FF_TILE)


def kernel(x, norm_mix_g, w_in, b_gate, q_norm_g, k_norm_g, lambda_q1, lambda_k1, lambda_q2, lambda_k2, subln_g, ssm_a_re, ssm_a_im, ssm_log_dt, ssm_b_re, ssm_b_im, ssm_c_re, ssm_c_im, ssm_d, w_glu, b_glu, w_proj_attn, w_proj_ssm, w_out, norm_mlp_g, w_mlp_in, w_mlp_out):
    for l in range(norm_mix_g.shape[0]):
        x = _layer(
            x, l, norm_mix_g[l], w_in[l], b_gate[l], q_norm_g[l], k_norm_g[l],
            lambda_q1[l], lambda_k1[l], lambda_q2[l], lambda_k2[l], subln_g[l],
            ssm_a_re[l], ssm_a_im[l], ssm_log_dt[l], ssm_b_re[l], ssm_b_im[l],
            ssm_c_re[l], ssm_c_im[l], ssm_d[l], w_glu[l], b_glu[l],
            w_proj_attn[l], w_proj_ssm[l], w_out[l],
            norm_mlp_g[l], w_mlp_in[l], w_mlp_out[l])
    return x
```

```python
import functools
import math

import jax
import jax.numpy as jnp
from jax import lax
from jax.experimental import pallas as pl
from jax.experimental.pallas import tpu as pltpu

F32 = jnp.float32
BF16 = jnp.bfloat16

EPS = 1e-6
CHUNK = 64
ATTN_HEADS = 4
ATTN_QK_DIM = 64
ATTN_V_DIM = 2 * ATTN_QK_DIM
SSM_GROUP = 16
SSM_STATE = 64
LANES = 128
SSM_T = LANES
V7X_VMEM_LIMIT_BYTES = 56 * 1024 * 1024

TOKEN_TILE = 512
ATTN_TILE = 512
MLP_FF_TILE = 1024


def _const_spec(shape):
    zeros = (0,) * len(shape)
    return pl.BlockSpec(shape, lambda *_: zeros, pipeline_mode=pl.Buffered(1))


def _in_proj_kernel(x_ref, g_ref, w_ref, bd_ref, qg_ref, kg_ref, bg_ref,
                    q_ref, k_ref, vt_ref, ut_ref, gate_ref, vu_ref, *, qk_w, v_w, u_w):
    x = x_ref[0]
    ms = jnp.mean(x * x, axis=-1, keepdims=True)
    h = (x * lax.rsqrt(ms + EPS) * g_ref[...]).astype(BF16)

    def proj(lo, hi):
        return jnp.dot(h, w_ref[:, lo:hi], preferred_element_type=F32)

    def group_norm(t, gain):
        ms_g = jnp.dot((t * t).astype(BF16), bd_ref[...], preferred_element_type=F32)
        return t * lax.rsqrt(ms_g + EPS) * gain

    o1 = qk_w
    o2 = o1 + qk_w
    o3 = o2 + v_w
    o4 = o3 + u_w
    q_ref[0] = group_norm(proj(0, o1), qg_ref[...]).astype(BF16)
    k_ref[0] = group_norm(proj(o1, o2), kg_ref[...]).astype(BF16)
    vu_ref[...] = proj(o2, o4)
    for hd in range(v_w // ATTN_V_DIM):
        vt_ref[0, hd, 0] = vu_ref[:, hd * ATTN_V_DIM:(hd + 1) * ATTN_V_DIM].T.astype(BF16)
    for j in range(ut_ref.shape[0]):
        ut_ref[j] = vu_ref[j * SSM_T:(j + 1) * SSM_T, v_w:].T
    gate_ref[0] = jax.nn.sigmoid(proj(o4, w_ref.shape[1]) + bg_ref[...]).astype(BF16)


def _in_proj(x, g, w, bd, qg, kg, bg, *, qk_w, v_w, u_w, tm):
    bsz, seq, d = x.shape
    gate_w = w.shape[1] - 2 * qk_w - v_w - u_w
    heads = v_w // ATTN_V_DIM
    tok = lambda b, i: (b, i, 0)
    out_shapes = (
        jax.ShapeDtypeStruct((bsz, seq, qk_w), BF16),
        jax.ShapeDtypeStruct((bsz, seq, qk_w), BF16),
        jax.ShapeDtypeStruct((bsz, heads, seq // tm, ATTN_V_DIM, tm), BF16),
        jax.ShapeDtypeStruct((seq // SSM_T, bsz, u_w, SSM_T), F32),
        jax.ShapeDtypeStruct((bsz, seq, gate_w), BF16),
    )
    return pl.pallas_call(
        functools.partial(_in_proj_kernel, qk_w=qk_w, v_w=v_w, u_w=u_w),
        grid=(bsz, seq // tm),
        in_specs=[
            pl.BlockSpec((1, tm, d), tok),
            _const_spec(g.shape), _const_spec(w.shape), _const_spec(bd.shape),
            _const_spec(qg.shape), _const_spec(kg.shape), _const_spec(bg.shape),
        ],
        out_specs=(
            pl.BlockSpec((1, tm, qk_w), tok), pl.BlockSpec((1, tm, qk_w), tok),
            pl.BlockSpec((1, heads, 1, ATTN_V_DIM, tm), lambda b, i: (b, 0, i, 0, 0)),
            pl.BlockSpec((tm // SSM_T, None, u_w, SSM_T), lambda b, i: (i, b, 0, 0)),
            pl.BlockSpec((1, tm, gate_w), tok),
        ),
        out_shape=out_shapes,
        scratch_shapes=[pltpu.VMEM((tm, v_w + u_w), F32)],
        compiler_params=pltpu.CompilerParams(
            dimension_semantics=("parallel", "parallel"), vmem_limit_bytes=V7X_VMEM_LIMIT_BYTES),
        name="in_proj",
    )(x, g, w, bd, qg, kg, bg)


def _attn_kernel(lam_ref, q_ref, k_ref, vt_ref, sg_ref, o_ref,
                 qc_ref, m_ref, l_ref, al_ref, acc_ref, s_ref, p_ref, *, t):
    i = pl.program_id(2)
    q = q_ref[0]
    lane = lax.broadcasted_iota(jnp.int32, q.shape, 1)
    zero = jnp.zeros_like(q)
    qcat = jnp.concatenate([jnp.where(lane < ATTN_QK_DIM, q, zero),
                            jnp.where(lane >= ATTN_QK_DIM, q, zero)], axis=0)

    qc_ref[...] = qcat
    m_ref[...] = jnp.full(m_ref.shape, -jnp.inf, F32)
    l_ref[...] = jnp.zeros(l_ref.shape, F32)
    al_ref[...] = jnp.ones(al_ref.shape, F32)
    acc_ref[...] = jnp.zeros(acc_ref.shape, F32)
    p_ref[1] = jnp.zeros(p_ref.shape[1:], BF16)

    def scores(j, masked, slot):
        kt = k_ref[0, pl.ds(pl.multiple_of(j * t, t), t), :]
        s = lax.dot_general(kt, qc_ref[...], (((1,), (1,)), ((), ())),
                            preferred_element_type=F32)
        if masked:
            key_chunk = lax.broadcasted_iota(jnp.int32, s.shape, 0) // CHUNK
            qry = lax.broadcasted_iota(jnp.int32, s.shape, 1)
            qry_chunk = jnp.where(qry >= t, qry - t, qry) // CHUNK
            s = jnp.where(key_chunk <= qry_chunk, s, -jnp.inf)
        s_ref[slot] = s

    def softmax(slot):
        s = s_ref[slot]
        m_old = m_ref[...]
        m_new = jnp.maximum(m_old, jnp.max(s, axis=0, keepdims=True))
        alpha = jnp.exp2(m_old - m_new)
        p = jnp.exp2(s - m_new)
        l_ref[...] = alpha * l_ref[...] + jnp.sum(p, axis=0, keepdims=True)
        m_ref[...] = m_new
        al_ref[...] = alpha
        p_ref[slot] = p.astype(BF16)

    def values(j, slot):
        acc_ref[...] = al_ref[...] * acc_ref[...] + jnp.dot(
            vt_ref[0, 0, jnp.maximum(j, 0)], p_ref[slot],
            preferred_element_type=F32)

    def iteration(j, parity, masked_next):
        values(j - 1, 1 - parity)
        scores(j + 1, masked_next, 1 - parity)
        softmax(parity)

    def tail(parity):
        values(i - 1, 1 - parity)
        softmax(parity)
        values(i, parity)

    @pl.when(i > 0)
    def _():
        scores(0, False, 0)

        def body(r, carry):
            iteration(2 * r, 0, False)
            iteration(2 * r + 1, 1, False)
            return carry

        lax.fori_loop(0, (i - 1) // 2, body, 0)

    @pl.when(i % 2 == 1)
    def _():
        iteration(i - 1, 0, True)
        tail(1)

    @pl.when(jnp.logical_and(i % 2 == 0, i > 0))
    def _():
        iteration(i - 2, 0, False)
        iteration(i - 1, 1, True)
        tail(0)

    @pl.when(i == 0)
    def _():
        scores(0, True, 0)
        tail(0)

    o = acc_ref[...] * (1.0 / l_ref[...])
    o = o[:, :t] - lam_ref[0] * o[:, t:]
    ms = jnp.mean(o * o, axis=0, keepdims=True)
    o = o * lax.rsqrt(ms + EPS) * sg_ref[...]
    o_ref[0] = o.T.astype(BF16)


def _attention(lam, q, k, vt, sg, *, t):
    b, s, w = q.shape
    heads = w // ATTN_V_DIM
    nk = s // t
    return pl.pallas_call(
        functools.partial(_attn_kernel, t=t),
        grid=(b, heads, nk),
        in_specs=[
            pl.BlockSpec(memory_space=pltpu.SMEM),
            pl.BlockSpec((1, t, ATTN_V_DIM), lambda bi, hi, i: (bi, i, hi)),
            pl.BlockSpec((1, s, ATTN_V_DIM), lambda bi, hi, i: (bi, 0, hi)),
            pl.BlockSpec((1, 1, nk, ATTN_V_DIM, t), lambda bi, hi, i: (bi, hi, 0, 0, 0)),
            _const_spec(sg.shape),
        ],
        out_specs=pl.BlockSpec((1, t, ATTN_V_DIM), lambda bi, hi, i: (bi, i, hi)),
        out_shape=jax.ShapeDtypeStruct((b, s, w), BF16),
        scratch_shapes=[
            pltpu.VMEM((2 * t, ATTN_V_DIM), BF16),
            pltpu.VMEM((1, 2 * t), F32),
            pltpu.VMEM((1, 2 * t), F32),
            pltpu.VMEM((1, 2 * t), F32),
            pltpu.VMEM((ATTN_V_DIM, 2 * t), F32),
            pltpu.VMEM((2, t, 2 * t), F32),
            pltpu.VMEM((2, t, 2 * t), BF16),
        ],
        compiler_params=pltpu.CompilerParams(
            dimension_semantics=("parallel", "parallel", "arbitrary"),
            vmem_limit_bytes=V7X_VMEM_LIMIT_BYTES),
        name="diff_attn",
    )(lam, q, k, vt, sg)


def _ssm_kernel(*refs, nb, nc):
    ng = SSM_GROUP
    (u_ref, ca_ref, cb_ref, ba_ref, bb_ref, bas_ref, bbs_ref, b2_ref, d_ref,
     pa0_ref, pb0_ref, pa1_ref, pb1_ref, qa_ref, qb_ref, lam_ref,
     y_ref, toep_ref, kr_ref, xc_ref, xcs_ref, x0_ref) = refs
    t = SSM_T
    gw = ng * t

    def rep(ref):
        v = ref[0]
        return jnp.broadcast_to(v[:, None, :], (ng, t, v.shape[-1])).reshape(gw, v.shape[-1])

    def tile(ref):
        v = ref[0]
        return jnp.broadcast_to(v[None], (ng, t, v.shape[-1])).reshape(gw, v.shape[-1])

    rca, rcb = rep(ca_ref), rep(cb_ref)
    cl0 = rca * tile(pa0_ref) + rcb * tile(pb0_ref)
    hm = (rca * tile(pa1_ref) + rcb * tile(pb1_ref)).astype(BF16)
    tqa, tqb = tile(qa_ref), tile(qb_ref)
    gm = (rep(ba_ref) * tqa + rep(bb_ref) * tqb).astype(BF16)
    gms = (rep(bas_ref) * tqa + rep(bbs_ref) * tqb).astype(BF16)

    kr = lax.dot_general(b2_ref[0], cl0, (((1,), (1,)), ((), ())),
                         precision=lax.Precision.HIGHEST, preferred_element_type=F32)
    lane = lax.broadcasted_iota(jnp.int32, kr.shape, 1)
    rowi = lax.broadcasted_iota(jnp.int32, kr.shape, 0)
    dfull = jnp.concatenate([d_ref[0]] * ng, axis=1)
    kr_ref[...] = kr + jnp.where(lane == rowi * t, dfull, 0.0)

    causal = (lax.broadcasted_iota(jnp.int32, (t, t), 1)
              >= lax.broadcasted_iota(jnp.int32, (t, t), 0))

    def fill(n_in, carry):
        row = kr_ref[pl.ds(n_in, 1), :]
        r0 = pl.multiple_of(n_in * t, t)
        for n_out in range(ng):
            seg = jnp.broadcast_to(row[:, n_out * t:(n_out + 1) * t], (t, t))
            blk = pltpu.roll(seg, 0, 1, stride=1, stride_axis=0)
            toep_ref[pl.ds(r0, t), n_out * t:(n_out + 1) * t] = (
                jnp.where(causal, blk, 0.0).astype(BF16))
        return carry

    lax.fori_loop(0, ng, fill, 0)

    u = jnp.concatenate([u_ref[:, n, :] for n in range(ng)], axis=1).astype(BF16)
    xc_ref[...] = jnp.dot(u, gm, preferred_element_type=F32)
    xcs_ref[...] = jnp.dot(u, gms, preferred_element_type=F32)

    lam = lam_ref[0]
    a, b, c = lam[0:1], lam[1:2], lam[2:3]

    def body(ci, carry):
        st, st_sw = carry
        r = pl.multiple_of(ci * nb, nb)
        x0_ref[pl.ds(r, nb), :] = st
        return (st * a + st_sw * b + xc_ref[pl.ds(r, nb), :],
                st_sw * a + st * c + xcs_ref[pl.ds(r, nb), :])

    z = jnp.zeros((nb, 2 * SSM_STATE), F32)
    lax.fori_loop(0, nc, body, (z, z))

    x0 = x0_ref[...]
    x0_hi = x0.astype(BF16)
    x0_lo = (x0 - x0_hi.astype(F32)).astype(BF16)
    nt = (((1,), (1,)), ((), ()))
    y = jnp.dot(u, toep_ref[...], preferred_element_type=F32)
    y = y + lax.dot_general(x0_hi, hm, nt, preferred_element_type=F32)
    y = y + lax.dot_general(x0_lo, hm, nt, preferred_element_type=F32)
    for n_out in range(ng):
        y_ref[:, n_out, :] = y[:, n_out * t:(n_out + 1) * t]


def _ssm(ut, tables, *, nb, nc):
    rows, width, t = ut.shape
    ng = SSM_GROUP
    groups = width // ng
    st2 = 2 * SSM_STATE
    gw = ng * t
    grp_spec = pl.BlockSpec((rows, ng, t), lambda g: (0, g, 0))
    tab_specs = [pl.BlockSpec((1,) + tab.shape[1:], lambda g: (g, 0, 0)) for tab in tables]
    return pl.pallas_call(
        functools.partial(_ssm_kernel, nb=nb, nc=nc),
        grid=(groups,),
        in_specs=[grp_spec] + tab_specs,
        out_specs=grp_spec,
        out_shape=jax.ShapeDtypeStruct((rows, width, t), F32),
        scratch_shapes=[
            pltpu.VMEM((gw, gw), BF16),
            pltpu.VMEM((ng, gw), F32),
            pltpu.VMEM((rows, st2), F32),
            pltpu.VMEM((rows, st2), F32),
            pltpu.VMEM((rows, st2), F32),
        ],
        compiler_params=pltpu.CompilerParams(
            dimension_semantics=("parallel",), vmem_limit_bytes=V7X_VMEM_LIMIT_BYTES),
        name="s5_ssm",
    )(ut, *tables)


def _ssm_tables(a_re, a_im, log_dt, b_re, b_im, c_re, c_im, d, t):
    dt = jnp.exp(log_dt)[:, None]
    lre = a_re * dt
    ang = a_im * dt
    mag = jnp.exp(lre)
    lb_re, lb_im = mag * jnp.cos(ang), mag * jnp.sin(ang)
    den = a_re * a_re + a_im * a_im
    nr, ni = lb_re - 1.0, lb_im
    f_re = (nr * a_re + ni * a_im) / den
    f_im = (ni * a_re - nr * a_im) / den
    bb_re = f_re[..., None] * b_re - f_im[..., None] * b_im
    bb_im = f_re[..., None] * b_im + f_im[..., None] * b_re
    bt_re, bt_im = jnp.swapaxes(bb_re, 1, 2), jnp.swapaxes(bb_im, 1, 2)
    cat = lambda x, y: jnp.concatenate([x, y], axis=-1)
    k = jnp.arange(t + 1, dtype=F32)[None, :, None]
    pmag = jnp.exp(k * lre[:, None, :])
    p_re, p_im = pmag * jnp.cos(k * ang[:, None, :]), pmag * jnp.sin(k * ang[:, None, :])
    pa, pb = cat(p_re, p_re), cat(p_im, p_im)
    tr, ti = p_re[:, t], p_im[:, t]
    lam_t = jnp.stack([cat(tr, tr), cat(-ti, ti), cat(ti, -ti)], axis=1)
    lam_t = jnp.concatenate([lam_t, jnp.zeros((lam_t.shape[0], 5, lam_t.shape[2]), F32)], axis=1)
    g = a_re.shape[0]
    d_tab = jnp.broadcast_to(d.reshape(g, SSM_GROUP, 1), (g, SSM_GROUP, t))
    return (
        cat(c_re, -c_im), cat(-c_im, -c_re),
        cat(bt_re, bt_im), cat(-bt_im, bt_re),
        cat(bt_im, bt_re), cat(bt_re, -bt_im),
        cat(bt_re, bt_im),
        d_tab,
        pa[:, :t], pb[:, :t],
        pa[:, 1:], pb[:, 1:],
        pa[:, t - 1::-1][:, :t], pb[:, t - 1::-1][:, :t],
        lam_t,
    )


def _merge_mlp_kernel(x_ref, a_ref, yt_ref, gate_ref, wglu_ref, bglu_ref, wpa_ref, wps_ref,
                      wout_ref, gm_ref, wmi_ref, wmo_ref, o_ref, *, ff_tile):
    d = x_ref.shape[2]
    y = jnp.concatenate([yt_ref[j].T for j in range(yt_ref.shape[0])], axis=0)
    z = jax.nn.gelu(y, approximate=True)
    glu = jnp.dot(z.astype(BF16), wglu_ref[...], preferred_element_type=F32) + bglu_ref[...]
    s = z * jax.nn.sigmoid(glu)
    pa = jnp.dot(a_ref[0], wpa_ref[...], preferred_element_type=F32)
    ps = jnp.dot(s.astype(BF16), wps_ref[...], preferred_element_type=F32)
    merged = gate_ref[0, :, :d].astype(F32) * pa + gate_ref[0, :, d:].astype(F32) * ps
    x1 = x_ref[0] + jnp.dot(merged.astype(BF16), wout_ref[...], preferred_element_type=F32)
    ms = jnp.mean(x1 * x1, axis=-1, keepdims=True)
    hm = (x1 * lax.rsqrt(ms + EPS) * gm_ref[...]).astype(BF16)
    acc = x1
    for j in range(wmi_ref.shape[1] // ff_tile):
        hj = jnp.dot(hm, wmi_ref[:, j * ff_tile:(j + 1) * ff_tile], preferred_element_type=F32)
        hj = jnp.square(jnp.maximum(hj, 0.0)).astype(BF16)
        acc = acc + jnp.dot(hj, wmo_ref[j * ff_tile:(j + 1) * ff_tile, :],
                            preferred_element_type=F32)
    o_ref[0] = acc


def _merge_mlp(x, a, yt, gates, wglu, bglu, wpa, wps, wout, gmlp, wmi, wmo, *, tm, ff_tile):
    bsz, seq, d = x.shape
    tok = lambda b, i: (b, i, 0)
    consts = (wglu, bglu, wpa, wps, wout, gmlp, wmi, wmo)
    return pl.pallas_call(
        functools.partial(_merge_mlp_kernel, ff_tile=ff_tile),
        grid=(bsz, seq // tm),
        in_specs=[
            pl.BlockSpec((1, tm, d), tok), pl.BlockSpec((1, tm, a.shape[2]), tok),
            pl.BlockSpec((tm // SSM_T, None, yt.shape[2], SSM_T), lambda b, i: (i, b, 0, 0)),
            pl.BlockSpec((1, tm, gates.shape[2]), tok),
        ] + [_const_spec(c.shape) for c in consts],
        out_specs=pl.BlockSpec((1, tm, d), tok),
        out_shape=jax.ShapeDtypeStruct((bsz, seq, d), F32),
        compiler_params=pltpu.CompilerParams(
            dimension_semantics=("parallel", "parallel"), vmem_limit_bytes=V7X_VMEM_LIMIT_BYTES),
        name="merge_mlp",
    )(x, a, yt, gates, *consts)


def _lambda_init(layer_idx):
    return 0.8 - 0.6 * math.exp(-0.3 * layer_idx)


def _layer(x, layer_idx, norm_mix_g, w_in, b_gate, q_norm_g, k_norm_g, lambda_q1, lambda_k1,
           lambda_q2, lambda_k2, subln_g, ssm_a_re, ssm_a_im, ssm_log_dt, ssm_b_re, ssm_b_im,
           ssm_c_re, ssm_c_im, ssm_d, w_glu, b_glu, w_proj_attn, w_proj_ssm, w_out,
           norm_mlp_g, w_mlp_in, w_mlp_out):
    bsz, seq, d = x.shape
    qk_w = ATTN_HEADS * 2 * ATTN_QK_DIM
    v_w = ATTN_HEADS * ATTN_V_DIM
    u_w = ssm_d.shape[0]
    row = lambda v: v.reshape(1, -1).astype(F32)
    tm = min(TOKEN_TILE, seq)
    t = min(ATTN_TILE, seq)
    assert tm == t and seq % tm == 0 and tm % SSM_T == 0

    group_mean = jnp.kron(jnp.eye(qk_w // ATTN_QK_DIM, dtype=F32),
                          jnp.full((ATTN_QK_DIM, ATTN_QK_DIM), 1.0 / ATTN_QK_DIM, F32)).astype(BF16)
    reps = qk_w // ATTN_QK_DIM
    q_gain = row(jnp.tile(q_norm_g.astype(F32) * (ATTN_QK_DIM ** -0.5 * math.log2(math.e)), reps))
    k_gain = row(jnp.tile(k_norm_g.astype(F32), reps))
    q, k, vt, ut, gates = _in_proj(
        x, row(norm_mix_g), w_in.astype(BF16), group_mean, q_gain, k_gain, row(b_gate),
        qk_w=qk_w, v_w=v_w, u_w=u_w, tm=tm)

    lam_init = _lambda_init(layer_idx)
    lam = (jnp.exp(jnp.sum(lambda_q1.astype(F32) * lambda_k1.astype(F32)))
           - jnp.exp(jnp.sum(lambda_q2.astype(F32) * lambda_k2.astype(F32))) + lam_init)
    sub_gain = (subln_g.astype(F32) * (1.0 - lam_init)).reshape(-1, 1)
    a = _attention(lam.reshape(1), q, k, vt, sub_gain, t=t)

    nc = seq // SSM_T
    tables = _ssm_tables(
        ssm_a_re.astype(F32), ssm_a_im.astype(F32), ssm_log_dt.astype(F32),
        ssm_b_re.astype(F32), ssm_b_im.astype(F32), ssm_c_re.astype(F32), ssm_c_im.astype(F32),
        ssm_d.astype(F32), SSM_T)
    yt = _ssm(ut.reshape(nc * bsz, u_w, SSM_T), tables, nb=bsz, nc=nc)
    yt = yt.reshape(nc, bsz, u_w, SSM_T)

    return _merge_mlp(
        x, a, yt, gates, w_glu.astype(BF16), row(b_glu),
        w_proj_attn.astype(BF16), w_proj_ssm.astype(BF16), w_out.astype(BF16),
        row(norm_mlp_g), w_mlp_in.astype(BF16), w_mlp_out.astype(BF16),
        tm=tm, ff_tile=MLP_FF_TILE)


def kernel(x, norm_mix_g, w_in, b_gate, q_norm_g, k_norm_g, lambda_q1, lambda_k1, lambda_q2, lambda_k2, subln_g, ssm_a_re, ssm_a_im, ssm_log_dt, ssm_b_re, ssm_b_im, ssm_c_re, ssm_c_im, ssm_d, w_glu, b_glu, w_proj_attn, w_proj_ssm, w_out, norm_mlp_g, w_mlp_in, w_mlp_out):
    for l in range(norm_mix_g.shape[0]):
        x = _layer(
            x, l, norm_mix_g[l], w_in[l], b_gate[l], q_norm_g[l], k_norm_g[l],
            lambda_q1[l], lambda_k1[l], lambda_q2[l], lambda_k2[l], subln_g[l],
            ssm_a_re[l], ssm_a_im[l], ssm_log_dt[l], ssm_b_re[l], ssm_b_im[l],
            ssm_c_re[l], ssm_c_im[l], ssm_d[l], w_glu[l], b_glu[l],
            w_proj_attn[l], w_proj_ssm[l], w_out[l],
            norm_mlp_g[l], w_mlp_in[l], w_mlp_out[l])
    return x
```

```python
import functools
import math

import jax
import jax.numpy as jnp
from jax import lax
from jax.experimental import pallas as pl
from jax.experimental.pallas import tpu as pltpu

F32 = jnp.float32
BF16 = jnp.bfloat16

EPS = 1e-6
CHUNK = 64
ATTN_HEADS = 4
ATTN_QK_DIM = 64
ATTN_V_DIM = 2 * ATTN_QK_DIM
SSM_GROUP = 16
SSM_STATE = 64
LANES = 128
SSM_T = LANES
V7X_VMEM_LIMIT_BYTES = 56 * 1024 * 1024

TOKEN_TILE = 512
ATTN_Q_TILE = 512
ATTN_K_TILE = 256
MLP_FF_TILE = 1024


def _const_spec(shape):
    zeros = (0,) * len(shape)
    return pl.BlockSpec(shape, lambda *_: zeros, pipeline_mode=pl.Buffered(1))


def _in_proj_kernel(x_ref, g_ref, w_ref, bd_ref, qg_ref, kg_ref, bg_ref,
                    q_ref, k_ref, vt_ref, ut_ref, gate_ref, vu_ref, *, qk_w, v_w, u_w):
    x = x_ref[0]
    ms = jnp.mean(x * x, axis=-1, keepdims=True)
    h = (x * lax.rsqrt(ms + EPS) * g_ref[...]).astype(BF16)

    def proj(lo, hi):
        return jnp.dot(h, w_ref[:, lo:hi], preferred_element_type=F32)

    def group_norm(t, gain):
        ms_g = jnp.dot((t * t).astype(BF16), bd_ref[...], preferred_element_type=F32)
        return t * lax.rsqrt(ms_g + EPS) * gain

    o1 = qk_w
    o2 = o1 + qk_w
    o3 = o2 + v_w
    o4 = o3 + u_w
    q_ref[0] = group_norm(proj(0, o1), qg_ref[...]).astype(BF16)
    k_ref[0] = group_norm(proj(o1, o2), kg_ref[...]).astype(BF16)
    vu_ref[...] = proj(o2, o4)
    tk = vt_ref.shape[-1]
    for hd in range(v_w // ATTN_V_DIM):
        for jk in range(vt_ref.shape[2]):
            vt_ref[0, hd, jk] = vu_ref[jk * tk:(jk + 1) * tk,
                                       hd * ATTN_V_DIM:(hd + 1) * ATTN_V_DIM].T.astype(BF16)
    for j in range(ut_ref.shape[0]):
        ut_ref[j] = vu_ref[j * SSM_T:(j + 1) * SSM_T, v_w:].T
    gate_ref[0] = jax.nn.sigmoid(proj(o4, w_ref.shape[1]) + bg_ref[...]).astype(BF16)


def _in_proj(x, g, w, bd, qg, kg, bg, *, qk_w, v_w, u_w, tm, tk):
    bsz, seq, d = x.shape
    gate_w = w.shape[1] - 2 * qk_w - v_w - u_w
    heads = v_w // ATTN_V_DIM
    tok = lambda b, i: (b, i, 0)
    out_shapes = (
        jax.ShapeDtypeStruct((bsz, seq, qk_w), BF16),
        jax.ShapeDtypeStruct((bsz, seq, qk_w), BF16),
        jax.ShapeDtypeStruct((bsz, heads, seq // tk, ATTN_V_DIM, tk), BF16),
        jax.ShapeDtypeStruct((seq // SSM_T, bsz, u_w, SSM_T), F32),
        jax.ShapeDtypeStruct((bsz, seq, gate_w), BF16),
    )
    return pl.pallas_call(
        functools.partial(_in_proj_kernel, qk_w=qk_w, v_w=v_w, u_w=u_w),
        grid=(bsz, seq // tm),
        in_specs=[
            pl.BlockSpec((1, tm, d), tok),
            _const_spec(g.shape), _const_spec(w.shape), _const_spec(bd.shape),
            _const_spec(qg.shape), _const_spec(kg.shape), _const_spec(bg.shape),
        ],
        out_specs=(
            pl.BlockSpec((1, tm, qk_w), tok), pl.BlockSpec((1, tm, qk_w), tok),
            pl.BlockSpec((1, heads, tm // tk, ATTN_V_DIM, tk), lambda b, i: (b, 0, i, 0, 0)),
            pl.BlockSpec((tm // SSM_T, None, u_w, SSM_T), lambda b, i: (i, b, 0, 0)),
            pl.BlockSpec((1, tm, gate_w), tok),
        ),
        out_shape=out_shapes,
        scratch_shapes=[pltpu.VMEM((tm, v_w + u_w), F32)],
        compiler_params=pltpu.CompilerParams(
            dimension_semantics=("parallel", "parallel"), vmem_limit_bytes=V7X_VMEM_LIMIT_BYTES),
        name="in_proj",
    )(x, g, w, bd, qg, kg, bg)


def _attn_kernel(lam_ref, q_ref, k_ref, vt_ref, sg_ref, o_ref,
                 qc_ref, m_ref, l_ref, al_ref, acc_ref, s_ref, p_ref, *, tq, tk):
    i = pl.program_id(2)
    r = tq // tk
    q = q_ref[0]
    lane = lax.broadcasted_iota(jnp.int32, q.shape, 1)
    zero = jnp.zeros_like(q)
    qc_ref[...] = jnp.concatenate([jnp.where(lane < ATTN_QK_DIM, q, zero),
                                   jnp.where(lane >= ATTN_QK_DIM, q, zero)], axis=0)
    m_ref[...] = jnp.full(m_ref.shape, -jnp.inf, F32)
    l_ref[...] = jnp.zeros(l_ref.shape, F32)
    al_ref[...] = jnp.ones(al_ref.shape, F32)
    acc_ref[...] = jnp.zeros(acc_ref.shape, F32)
    p_ref[1] = jnp.zeros(p_ref.shape[1:], BF16)

    def scores(j, masked, slot):
        kt = k_ref[0, pl.ds(pl.multiple_of(j * tk, tk), tk), :]
        s = lax.dot_general(kt, qc_ref[...], (((1,), (1,)), ((), ())),
                            preferred_element_type=F32)
        if masked:
            key_chunk = (lax.broadcasted_iota(jnp.int32, s.shape, 0) // CHUNK
                         + (j * (tk // CHUNK) - i * (tq // CHUNK)))
            qry = lax.broadcasted_iota(jnp.int32, s.shape, 1)
            qry_chunk = jnp.where(qry >= tq, qry - tq, qry) // CHUNK
            s = jnp.where(key_chunk <= qry_chunk, s, -jnp.inf)
        s_ref[slot] = s

    def softmax(slot):
        s = s_ref[slot]
        m_old = m_ref[...]
        m_new = jnp.maximum(m_old, jnp.max(s, axis=0, keepdims=True))
        alpha = jnp.exp2(m_old - m_new)
        p = jnp.exp2(s - m_new)
        l_ref[...] = alpha * l_ref[...] + jnp.sum(p, axis=0, keepdims=True)
        m_ref[...] = m_new
        al_ref[...] = alpha
        p_ref[slot] = p.astype(BF16)

    def values(j, slot):
        acc_ref[...] = al_ref[...] * acc_ref[...] + jnp.dot(
            vt_ref[0, 0, jnp.maximum(j, 0)], p_ref[slot],
            preferred_element_type=F32)

    def iteration(j, parity, masked_next):
        values(j - 1, 1 - parity)
        scores(j + 1, masked_next, 1 - parity)
        softmax(parity)

    first_masked = r * i

    @pl.when(i > 0)
    def _():
        scores(0, False, 0)

        def body(h, carry):
            iteration(2 * h, 0, False)
            iteration(2 * h + 1, 1, False)
            return carry

        lax.fori_loop(0, first_masked // 2 - 1, body, 0)
        iteration(first_masked - 2, 0, False)
        iteration(first_masked - 1, 1, True)

    @pl.when(i == 0)
    def _():
        scores(0, True, 0)

    for jj in range(r - 1):
        iteration(first_masked + jj, jj % 2, True)
    last = first_masked + r - 1
    values(last - 1, (r - 2) % 2)
    softmax((r - 1) % 2)
    values(last, (r - 1) % 2)

    o = acc_ref[...] * (1.0 / l_ref[...])
    o = o[:, :tq] - lam_ref[0] * o[:, tq:]
    ms = jnp.mean(o * o, axis=0, keepdims=True)
    o = o * lax.rsqrt(ms + EPS) * sg_ref[...]
    o_ref[0] = o.T.astype(BF16)


def _attention(lam, q, k, vt, sg, *, tq, tk):
    b, s, w = q.shape
    heads = w // ATTN_V_DIM
    assert tq % tk == 0 and (tq // tk) % 2 == 0 and tk % CHUNK == 0
    return pl.pallas_call(
        functools.partial(_attn_kernel, tq=tq, tk=tk),
        grid=(b, heads, s // tq),
        in_specs=[
            pl.BlockSpec(memory_space=pltpu.SMEM),
            pl.BlockSpec((1, tq, ATTN_V_DIM), lambda bi, hi, i: (bi, i, hi)),
            pl.BlockSpec((1, s, ATTN_V_DIM), lambda bi, hi, i: (bi, 0, hi)),
            pl.BlockSpec((1, 1, s // tk, ATTN_V_DIM, tk), lambda bi, hi, i: (bi, hi, 0, 0, 0)),
            _const_spec(sg.shape),
        ],
        out_specs=pl.BlockSpec((1, tq, ATTN_V_DIM), lambda bi, hi, i: (bi, i, hi)),
        out_shape=jax.ShapeDtypeStruct((b, s, w), BF16),
        scratch_shapes=[
            pltpu.VMEM((2 * tq, ATTN_V_DIM), BF16),
            pltpu.VMEM((1, 2 * tq), F32),
            pltpu.VMEM((1, 2 * tq), F32),
            pltpu.VMEM((1, 2 * tq), F32),
            pltpu.VMEM((ATTN_V_DIM, 2 * tq), F32),
            pltpu.VMEM((2, tk, 2 * tq), F32),
            pltpu.VMEM((2, tk, 2 * tq), BF16),
        ],
        compiler_params=pltpu.CompilerParams(
            dimension_semantics=("parallel", "parallel", "arbitrary"),
            vmem_limit_bytes=V7X_VMEM_LIMIT_BYTES),
        name="diff_attn",
    )(lam, q, k, vt, sg)


def _ssm_kernel(*refs, nb, nc):
    ng = SSM_GROUP
    (u_ref, ca_ref, cb_ref, ba_ref, bb_ref, bas_ref, bbs_ref, b2_ref, d_ref,
     pa0_ref, pb0_ref, pa1_ref, pb1_ref, qa_ref, qb_ref, lam_ref,
     y_ref, toep_ref, kr_ref, xc_ref, xcs_ref, x0_ref) = refs
    t = SSM_T
    gw = ng * t

    def rep(ref):
        v = ref[0]
        return jnp.broadcast_to(v[:, None, :], (ng, t, v.shape[-1])).reshape(gw, v.shape[-1])

    def tile(ref):
        v = ref[0]
        return jnp.broadcast_to(v[None], (ng, t, v.shape[-1])).reshape(gw, v.shape[-1])

    rca, rcb = rep(ca_ref), rep(cb_ref)
    cl0 = rca * tile(pa0_ref) + rcb * tile(pb0_ref)
    hm = (rca * tile(pa1_ref) + rcb * tile(pb1_ref)).astype(BF16)
    tqa, tqb = tile(qa_ref), tile(qb_ref)
    gm = (rep(ba_ref) * tqa + rep(bb_ref) * tqb).astype(BF16)
    gms = (rep(bas_ref) * tqa + rep(bbs_ref) * tqb).astype(BF16)

    kr = lax.dot_general(b2_ref[0], cl0, (((1,), (1,)), ((), ())),
                         precision=lax.Precision.HIGHEST, preferred_element_type=F32)
    lane = lax.broadcasted_iota(jnp.int32, kr.shape, 1)
    rowi = lax.broadcasted_iota(jnp.int32, kr.shape, 0)
    dfull = jnp.concatenate([d_ref[0]] * ng, axis=1)
    kr_ref[...] = kr + jnp.where(lane == rowi * t, dfull, 0.0)

    causal = (lax.broadcasted_iota(jnp.int32, (t, t), 1)
              >= lax.broadcasted_iota(jnp.int32, (t, t), 0))

    def fill(n_in, carry):
        row = kr_ref[pl.ds(n_in, 1), :]
        r0 = pl.multiple_of(n_in * t, t)
        for n_out in range(ng):
            seg = jnp.broadcast_to(row[:, n_out * t:(n_out + 1) * t], (t, t))
            blk = pltpu.roll(seg, 0, 1, stride=1, stride_axis=0)
            toep_ref[pl.ds(r0, t), n_out * t:(n_out + 1) * t] = (
                jnp.where(causal, blk, 0.0).astype(BF16))
        return carry

    lax.fori_loop(0, ng, fill, 0)

    u = jnp.concatenate([u_ref[:, n, :] for n in range(ng)], axis=1).astype(BF16)
    xc_ref[...] = jnp.dot(u, gm, preferred_element_type=F32)
    xcs_ref[...] = jnp.dot(u, gms, preferred_element_type=F32)

    lam = lam_ref[0]
    a, b, c = lam[0:1], lam[1:2], lam[2:3]

    def body(ci, carry):
        st, st_sw = carry
        r = pl.multiple_of(ci * nb, nb)
        x0_ref[pl.ds(r, nb), :] = st
        return (st * a + st_sw * b + xc_ref[pl.ds(r, nb), :],
                st_sw * a + st * c + xcs_ref[pl.ds(r, nb), :])

    z = jnp.zeros((nb, 2 * SSM_STATE), F32)
    lax.fori_loop(0, nc, body, (z, z))

    x0 = x0_ref[...]
    x0_hi = x0.astype(BF16)
    x0_lo = (x0 - x0_hi.astype(F32)).astype(BF16)
    nt = (((1,), (1,)), ((), ()))
    y = jnp.dot(u, toep_ref[...], preferred_element_type=F32)
    y = y + lax.dot_general(x0_hi, hm, nt, preferred_element_type=F32)
    y = y + lax.dot_general(x0_lo, hm, nt, preferred_element_type=F32)
    for n_out in range(ng):
        y_ref[:, n_out, :] = y[:, n_out * t:(n_out + 1) * t]


def _ssm(ut, tables, *, nb, nc):
    rows, width, t = ut.shape
    ng = SSM_GROUP
    groups = width // ng
    st2 = 2 * SSM_STATE
    gw = ng * t
    grp_spec = pl.BlockSpec((rows, ng, t), lambda g: (0, g, 0))
    tab_specs = [pl.BlockSpec((1,) + tab.shape[1:], lambda g: (g, 0, 0)) for tab in tables]
    return pl.pallas_call(
        functools.partial(_ssm_kernel, nb=nb, nc=nc),
        grid=(groups,),
        in_specs=[grp_spec] + tab_specs,
        out_specs=grp_spec,
        out_shape=jax.ShapeDtypeStruct((rows, width, t), F32),
        scratch_shapes=[
            pltpu.VMEM((gw, gw), BF16),
            pltpu.VMEM((ng, gw), F32),
            pltpu.VMEM((rows, st2), F32),
            pltpu.VMEM((rows, st2), F32),
            pltpu.VMEM((rows, st2), F32),
        ],
        compiler_params=pltpu.CompilerParams(
            dimension_semantics=("parallel",), vmem_limit_bytes=V7X_VMEM_LIMIT_BYTES),
        name="s5_ssm",
    )(ut, *tables)


def _ssm_tables(a_re, a_im, log_dt, b_re, b_im, c_re, c_im, d, t):
    dt = jnp.exp(log_dt)[:, None]
    lre = a_re * dt
    ang = a_im * dt
    mag = jnp.exp(lre)
    lb_re, lb_im = mag * jnp.cos(ang), mag * jnp.sin(ang)
    den = a_re * a_re + a_im * a_im
    nr, ni = lb_re - 1.0, lb_im
    f_re = (nr * a_re + ni * a_im) / den
    f_im = (ni * a_re - nr * a_im) / den
    bb_re = f_re[..., None] * b_re - f_im[..., None] * b_im
    bb_im = f_re[..., None] * b_im + f_im[..., None] * b_re
    bt_re, bt_im = jnp.swapaxes(bb_re, 1, 2), jnp.swapaxes(bb_im, 1, 2)
    cat = lambda x, y: jnp.concatenate([x, y], axis=-1)
    k = jnp.arange(t + 1, dtype=F32)[None, :, None]
    pmag = jnp.exp(k * lre[:, None, :])
    p_re, p_im = pmag * jnp.cos(k * ang[:, None, :]), pmag * jnp.sin(k * ang[:, None, :])
    pa, pb = cat(p_re, p_re), cat(p_im, p_im)
    tr, ti = p_re[:, t], p_im[:, t]
    lam_t = jnp.stack([cat(tr, tr), cat(-ti, ti), cat(ti, -ti)], axis=1)
    lam_t = jnp.concatenate([lam_t, jnp.zeros((lam_t.shape[0], 5, lam_t.shape[2]), F32)], axis=1)
    g = a_re.shape[0]
    d_tab = jnp.broadcast_to(d.reshape(g, SSM_GROUP, 1), (g, SSM_GROUP, t))
    return (
        cat(c_re, -c_im), cat(-c_im, -c_re),
        cat(bt_re, bt_im), cat(-bt_im, bt_re),
        cat(bt_im, bt_re), cat(bt_re, -bt_im),
        cat(bt_re, bt_im),
        d_tab,
        pa[:, :t], pb[:, :t],
        pa[:, 1:], pb[:, 1:],
        pa[:, t - 1::-1][:, :t], pb[:, t - 1::-1][:, :t],
        lam_t,
    )


def _merge_mlp_kernel(x_ref, a_ref, yt_ref, gate_ref, wglu_ref, bglu_ref, wpa_ref, wps_ref,
                      wout_ref, gm_ref, wmi_ref, wmo_ref, o_ref, *, ff_tile):
    d = x_ref.shape[2]
    y = jnp.concatenate([yt_ref[j].T for j in range(yt_ref.shape[0])], axis=0)
    z = jax.nn.gelu(y, approximate=True)
    glu = jnp.dot(z.astype(BF16), wglu_ref[...], preferred_element_type=F32) + bglu_ref[...]
    s = z * jax.nn.sigmoid(glu)
    pa = jnp.dot(a_ref[0], wpa_ref[...], preferred_element_type=F32)
    ps = jnp.dot(s.astype(BF16), wps_ref[...], preferred_element_type=F32)
    merged = gate_ref[0, :, :d].astype(F32) * pa + gate_ref[0, :, d:].astype(F32) * ps
    x1 = x_ref[0] + jnp.dot(merged.astype(BF16), wout_ref[...], preferred_element_type=F32)
    ms = jnp.mean(x1 * x1, axis=-1, keepdims=True)
    hm = (x1 * lax.rsqrt(ms + EPS) * gm_ref[...]).astype(BF16)
    acc = x1
    for j in range(wmi_ref.shape[1] // ff_tile):
        hj = jnp.dot(hm, wmi_ref[:, j * ff_tile:(j + 1) * ff_tile], preferred_element_type=F32)
        hj = jnp.square(jnp.maximum(hj, 0.0)).astype(BF16)
        acc = acc + jnp.dot(hj, wmo_ref[j * ff_tile:(j + 1) * ff_tile, :],
                            preferred_element_type=F32)
    o_ref[0] = acc


def _merge_mlp(x, a, yt, gates, wglu, bglu, wpa, wps, wout, gmlp, wmi, wmo, *, tm, ff_tile):
    bsz, seq, d = x.shape
    tok = lambda b, i: (b, i, 0)
    consts = (wglu, bglu, wpa, wps, wout, gmlp, wmi, wmo)
    return pl.pallas_call(
        functools.partial(_merge_mlp_kernel, ff_tile=ff_tile),
        grid=(bsz, seq // tm),
        in_specs=[
            pl.BlockSpec((1, tm, d), tok), pl.BlockSpec((1, tm, a.shape[2]), tok),
            pl.BlockSpec((tm // SSM_T, None, yt.shape[2], SSM_T), lambda b, i: (i, b, 0, 0)),
            pl.BlockSpec((1, tm, gates.shape[2]), tok),
        ] + [_const_spec(c.shape) for c in consts],
        out_specs=pl.BlockSpec((1, tm, d), tok),
        out_shape=jax.ShapeDtypeStruct((bsz, seq, d), F32),
        compiler_params=pltpu.CompilerParams(
            dimension_semantics=("parallel", "parallel"), vmem_limit_bytes=V7X_VMEM_LIMIT_BYTES),
        name="merge_mlp",
    )(x, a, yt, gates, *consts)


def _lambda_init(layer_idx):
    return 0.8 - 0.6 * math.exp(-0.3 * layer_idx)


def _layer(x, layer_idx, norm_mix_g, w_in, b_gate, q_norm_g, k_norm_g, lambda_q1, lambda_k1,
           lambda_q2, lambda_k2, subln_g, ssm_a_re, ssm_a_im, ssm_log_dt, ssm_b_re, ssm_b_im,
           ssm_c_re, ssm_c_im, ssm_d, w_glu, b_glu, w_proj_attn, w_proj_ssm, w_out,
           norm_mlp_g, w_mlp_in, w_mlp_out):
    bsz, seq, d = x.shape
    qk_w = ATTN_HEADS * 2 * ATTN_QK_DIM
    v_w = ATTN_HEADS * ATTN_V_DIM
    u_w = ssm_d.shape[0]
    row = lambda v: v.reshape(1, -1).astype(F32)
    tm = min(TOKEN_TILE, seq)
    tq, tk = min(ATTN_Q_TILE, seq), min(ATTN_K_TILE, seq)
    assert tm == tq and seq % tm == 0 and tm % SSM_T == 0

    group_mean = jnp.kron(jnp.eye(qk_w // ATTN_QK_DIM, dtype=F32),
                          jnp.full((ATTN_QK_DIM, ATTN_QK_DIM), 1.0 / ATTN_QK_DIM, F32)).astype(BF16)
    reps = qk_w // ATTN_QK_DIM
    q_gain = row(jnp.tile(q_norm_g.astype(F32) * (ATTN_QK_DIM ** -0.5 * math.log2(math.e)), reps))
    k_gain = row(jnp.tile(k_norm_g.astype(F32), reps))
    q, k, vt, ut, gates = _in_proj(
        x, row(norm_mix_g), w_in.astype(BF16), group_mean, q_gain, k_gain, row(b_gate),
        qk_w=qk_w, v_w=v_w, u_w=u_w, tm=tm, tk=tk)

    lam_init = _lambda_init(layer_idx)
    lam = (jnp.exp(jnp.sum(lambda_q1.astype(F32) * lambda_k1.astype(F32)))
           - jnp.exp(jnp.sum(lambda_q2.astype(F32) * lambda_k2.astype(F32))) + lam_init)
    sub_gain = (subln_g.astype(F32) * (1.0 - lam_init)).reshape(-1, 1)
    a = _attention(lam.reshape(1), q, k, vt, sub_gain, tq=tq, tk=tk)

    nc = seq // SSM_T
    tables = _ssm_tables(
        ssm_a_re.astype(F32), ssm_a_im.astype(F32), ssm_log_dt.astype(F32),
        ssm_b_re.astype(F32), ssm_b_im.astype(F32), ssm_c_re.astype(F32), ssm_c_im.astype(F32),
        ssm_d.astype(F32), SSM_T)
    yt = _ssm(ut.reshape(nc * bsz, u_w, SSM_T), tables, nb=bsz, nc=nc)
    yt = yt.reshape(nc, bsz, u_w, SSM_T)

    return _merge_mlp(
        x, a, yt, gates, w_glu.astype(BF16), row(b_glu),
        w_proj_attn.astype(BF16), w_proj_ssm.astype(BF16), w_out.astype(BF16),
        row(norm_mlp_g), w_mlp_in.astype(BF16), w_mlp_out.astype(BF16),
        tm=tm, ff_tile=MLP_FF_TILE)


def kernel(x, norm_mix_g, w_in, b_gate, q_norm_g, k_norm_g, lambda_q1, lambda_k1, lambda_q2, lambda_k2, subln_g, ssm_a_re, ssm_a_im, ssm_log_dt, ssm_b_re, ssm_b_im, ssm_c_re, ssm_c_im, ssm_d, w_glu, b_glu, w_proj_attn, w_proj_ssm, w_out, norm_mlp_g, w_mlp_in, w_mlp_out):
    for l in range(norm_mix_g.shape[0]):
        x = _layer(
            x, l, norm_mix_g[l], w_in[l], b_gate[l], q_norm_g[l], k_norm_g[l],
            lambda_q1[l], lambda_k1[l], lambda_q2[l], lambda_k2[l], subln_g[l],
            ssm_a_re[l], ssm_a_im[l], ssm_log_dt[l], ssm_b_re[l], ssm_b_im[l],
            ssm_c_re[l], ssm_c_im[l], ssm_d[l], w_glu[l], b_glu[l],
            w_proj_attn[l], w_proj_ssm[l], w_out[l],
            norm_mlp_g[l], w_mlp_in[l], w_mlp_out[l])
    return x
```

```python
import functools
import math

import jax
import jax.numpy as jnp
from jax import lax
from jax.experimental import pallas as pl
from jax.experimental.pallas import tpu as pltpu

F32 = jnp.float32
BF16 = jnp.bfloat16

EPS = 1e-6
CHUNK = 64
ATTN_HEADS = 4
ATTN_QK_DIM = 64
ATTN_V_DIM = 2 * ATTN_QK_DIM
SSM_GROUP = 16
SSM_STATE = 64
LANES = 128
SSM_T = LANES
V7X_VMEM_LIMIT_BYTES = 56 * 1024 * 1024

TOKEN_TILE = 512
ATTN_TILE = 512
ATTN_SCORE_LIMIT = 60.0
MLP_FF_TILE = 1024


def _const_spec(shape):
    zeros = (0,) * len(shape)
    return pl.BlockSpec(shape, lambda *_: zeros, pipeline_mode=pl.Buffered(1))


def _in_proj_kernel(x_ref, g_ref, w_ref, bd_ref, qg_ref, kg_ref, bg_ref,
                    q_ref, k_ref, vt_ref, ut_ref, gate_ref, vu_ref, *, qk_w, v_w, u_w):
    x = x_ref[0]
    ms = jnp.mean(x * x, axis=-1, keepdims=True)
    h = (x * lax.rsqrt(ms + EPS) * g_ref[...]).astype(BF16)

    def proj(lo, hi):
        return jnp.dot(h, w_ref[:, lo:hi], preferred_element_type=F32)

    def group_norm(t, gain):
        ms_g = jnp.dot((t * t).astype(BF16), bd_ref[...], preferred_element_type=F32)
        return t * lax.rsqrt(ms_g + EPS) * gain

    o1 = qk_w
    o2 = o1 + qk_w
    o3 = o2 + v_w
    o4 = o3 + u_w
    q_ref[0] = group_norm(proj(0, o1), qg_ref[...]).astype(BF16)
    k_ref[0] = group_norm(proj(o1, o2), kg_ref[...]).astype(BF16)
    vu_ref[...] = proj(o2, o4)
    for hd in range(v_w // ATTN_V_DIM):
        vt_ref[0, hd, 0] = vu_ref[:, hd * ATTN_V_DIM:(hd + 1) * ATTN_V_DIM].T.astype(BF16)
    for j in range(ut_ref.shape[0]):
        ut_ref[j] = vu_ref[j * SSM_T:(j + 1) * SSM_T, v_w:].T
    gate_ref[0] = jax.nn.sigmoid(proj(o4, w_ref.shape[1]) + bg_ref[...]).astype(BF16)


def _in_proj(x, g, w, bd, qg, kg, bg, *, qk_w, v_w, u_w, tm):
    bsz, seq, d = x.shape
    gate_w = w.shape[1] - 2 * qk_w - v_w - u_w
    heads = v_w // ATTN_V_DIM
    tok = lambda b, i: (b, i, 0)
    out_shapes = (
        jax.ShapeDtypeStruct((bsz, seq, qk_w), BF16),
        jax.ShapeDtypeStruct((bsz, seq, qk_w), BF16),
        jax.ShapeDtypeStruct((bsz, heads, seq // tm, ATTN_V_DIM, tm), BF16),
        jax.ShapeDtypeStruct((seq // SSM_T, bsz, u_w, SSM_T), F32),
        jax.ShapeDtypeStruct((bsz, seq, gate_w), BF16),
    )
    return pl.pallas_call(
        functools.partial(_in_proj_kernel, qk_w=qk_w, v_w=v_w, u_w=u_w),
        grid=(bsz, seq // tm),
        in_specs=[
            pl.BlockSpec((1, tm, d), tok),
            _const_spec(g.shape), _const_spec(w.shape), _const_spec(bd.shape),
            _const_spec(qg.shape), _const_spec(kg.shape), _const_spec(bg.shape),
        ],
        out_specs=(
            pl.BlockSpec((1, tm, qk_w), tok), pl.BlockSpec((1, tm, qk_w), tok),
            pl.BlockSpec((1, heads, 1, ATTN_V_DIM, tm), lambda b, i: (b, 0, i, 0, 0)),
            pl.BlockSpec((tm // SSM_T, None, u_w, SSM_T), lambda b, i: (i, b, 0, 0)),
            pl.BlockSpec((1, tm, gate_w), tok),
        ),
        out_shape=out_shapes,
        scratch_shapes=[pltpu.VMEM((tm, v_w + u_w), F32)],
        compiler_params=pltpu.CompilerParams(
            dimension_semantics=("parallel", "parallel"), vmem_limit_bytes=V7X_VMEM_LIMIT_BYTES),
        name="in_proj",
    )(x, g, w, bd, qg, kg, bg)


def _attn_kernel(ctl_ref, q_ref, k_ref, vt_ref, sg_ref, o_ref,
                 qc_ref, m_ref, l_ref, al_ref, acc_ref, l8_ref, s_ref, p_ref, *, t):
    i = pl.program_id(2)
    q = q_ref[0]
    lane = lax.broadcasted_iota(jnp.int32, q.shape, 1)
    zero = jnp.zeros_like(q)
    qc_ref[...] = jnp.concatenate([jnp.where(lane < ATTN_QK_DIM, q, zero),
                                   jnp.where(lane >= ATTN_QK_DIM, q, zero)], axis=0)
    acc_ref[...] = jnp.zeros(acc_ref.shape, F32)
    bounded = ctl_ref[1] > 0.0

    def raw_scores(j, masked):
        kt = k_ref[0, pl.ds(pl.multiple_of(j * t, t), t), :]
        s = lax.dot_general(kt, qc_ref[...], (((1,), (1,)), ((), ())),
                            preferred_element_type=F32)
        if masked:
            key_chunk = lax.broadcasted_iota(jnp.int32, s.shape, 0) // CHUNK
            qry = lax.broadcasted_iota(jnp.int32, s.shape, 1)
            qry_chunk = jnp.where(qry >= t, qry - t, qry) // CHUNK
            s = jnp.where(key_chunk <= qry_chunk, s, -jnp.inf)
        return s

    def pipeline(first, step, finish):
        @pl.when(i > 0)
        def _():
            first(0, False, 0)

            def body(h, carry):
                step(2 * h, 0, False)
                step(2 * h + 1, 1, False)
                return carry

            lax.fori_loop(0, (i - 1) // 2, body, 0)

        @pl.when(i % 2 == 1)
        def _():
            step(i - 1, 0, True)
            finish(i, 1)

        @pl.when(jnp.logical_and(i % 2 == 0, i > 0))
        def _():
            step(i - 2, 0, False)
            step(i - 1, 1, True)
            finish(i, 0)

        @pl.when(i == 0)
        def _():
            first(0, True, 0)
            finish(0, 0)

    def probs(j, masked, slot):
        p = jnp.exp2(raw_scores(j, masked))
        l8_ref[...] += jnp.sum(p.reshape(t // 8, 8, 2 * t), axis=0)
        p_ref[slot] = p.astype(BF16)

    def accumulate(j, slot):
        acc_ref[...] += jnp.dot(vt_ref[0, 0, j], p_ref[slot], preferred_element_type=F32)

    def bounded_step(j, slot, masked_next):
        accumulate(j, slot)
        probs(j + 1, masked_next, 1 - slot)

    @pl.when(bounded)
    def _():
        l8_ref[...] = jnp.zeros(l8_ref.shape, F32)
        pipeline(probs, bounded_step, accumulate)
        l_ref[...] = jnp.sum(l8_ref[...], axis=0, keepdims=True)

    def scores(j, masked, slot):
        s_ref[slot] = raw_scores(j, masked)

    def softmax(slot):
        s = s_ref[slot]
        m_old = m_ref[...]
        m_new = jnp.maximum(m_old, jnp.max(s, axis=0, keepdims=True))
        alpha = jnp.exp2(m_old - m_new)
        p = jnp.exp2(s - m_new)
        l_ref[...] = alpha * l_ref[...] + jnp.sum(p, axis=0, keepdims=True)
        m_ref[...] = m_new
        al_ref[...] = alpha
        p_ref[slot] = p.astype(BF16)

    def values(j, slot):
        acc_ref[...] = al_ref[...] * acc_ref[...] + jnp.dot(
            vt_ref[0, 0, jnp.maximum(j, 0)], p_ref[slot],
            preferred_element_type=F32)

    def online_step(j, slot, masked_next):
        values(j - 1, 1 - slot)
        scores(j + 1, masked_next, 1 - slot)
        softmax(slot)

    def online_finish(j, slot):
        values(j - 1, 1 - slot)
        softmax(slot)
        values(j, slot)

    @pl.when(jnp.logical_not(bounded))
    def _():
        m_ref[...] = jnp.full(m_ref.shape, -jnp.inf, F32)
        l_ref[...] = jnp.zeros(l_ref.shape, F32)
        al_ref[...] = jnp.ones(al_ref.shape, F32)
        p_ref[1] = jnp.zeros(p_ref.shape[1:], BF16)
        pipeline(scores, online_step, online_finish)

    o = acc_ref[...] * (1.0 / l_ref[...])
    o = o[:, :t] - ctl_ref[0] * o[:, t:]
    ms = jnp.mean(o * o, axis=0, keepdims=True)
    o = o * lax.rsqrt(ms + EPS) * sg_ref[...]
    o_ref[0] = o.T.astype(BF16)


def _attention(ctl, q, k, vt, sg, *, t):
    b, s, w = q.shape
    heads = w // ATTN_V_DIM
    nk = s // t
    return pl.pallas_call(
        functools.partial(_attn_kernel, t=t),
        grid=(b, heads, nk),
        in_specs=[
            pl.BlockSpec(memory_space=pltpu.SMEM),
            pl.BlockSpec((1, t, ATTN_V_DIM), lambda bi, hi, i: (bi, i, hi)),
            pl.BlockSpec((1, s, ATTN_V_DIM), lambda bi, hi, i: (bi, 0, hi)),
            pl.BlockSpec((1, 1, nk, ATTN_V_DIM, t), lambda bi, hi, i: (bi, hi, 0, 0, 0)),
            _const_spec(sg.shape),
        ],
        out_specs=pl.BlockSpec((1, t, ATTN_V_DIM), lambda bi, hi, i: (bi, i, hi)),
        out_shape=jax.ShapeDtypeStruct((b, s, w), BF16),
        scratch_shapes=[
            pltpu.VMEM((2 * t, ATTN_V_DIM), BF16),
            pltpu.VMEM((1, 2 * t), F32),
            pltpu.VMEM((1, 2 * t), F32),
            pltpu.VMEM((1, 2 * t), F32),
            pltpu.VMEM((ATTN_V_DIM, 2 * t), F32),
            pltpu.VMEM((8, 2 * t), F32),
            pltpu.VMEM((2, t, 2 * t), F32),
            pltpu.VMEM((2, t, 2 * t), BF16),
        ],
        compiler_params=pltpu.CompilerParams(
            dimension_semantics=("parallel", "parallel", "arbitrary"),
            vmem_limit_bytes=V7X_VMEM_LIMIT_BYTES),
        name="diff_attn",
    )(ctl, q, k, vt, sg)


def _ssm_kernel(*refs, nb, nc):
    ng = SSM_GROUP
    (u_ref, ca_ref, cb_ref, ba_ref, bb_ref, bas_ref, bbs_ref, b2_ref, d_ref,
     pa0_ref, pb0_ref, pa1_ref, pb1_ref, qa_ref, qb_ref, lam_ref,
     y_ref, toep_ref, kr_ref, xc_ref, xcs_ref, x0_ref) = refs
    t = SSM_T
    gw = ng * t

    def rep(ref):
        v = ref[0]
        return jnp.broadcast_to(v[:, None, :], (ng, t, v.shape[-1])).reshape(gw, v.shape[-1])

    def tile(ref):
        v = ref[0]
        return jnp.broadcast_to(v[None], (ng, t, v.shape[-1])).reshape(gw, v.shape[-1])

    rca, rcb = rep(ca_ref), rep(cb_ref)
    cl0 = rca * tile(pa0_ref) + rcb * tile(pb0_ref)
    hm = (rca * tile(pa1_ref) + rcb * tile(pb1_ref)).astype(BF16)
    tqa, tqb = tile(qa_ref), tile(qb_ref)
    gm = (rep(ba_ref) * tqa + rep(bb_ref) * tqb).astype(BF16)
    gms = (rep(bas_ref) * tqa + rep(bbs_ref) * tqb).astype(BF16)

    kr = lax.dot_general(b2_ref[0], cl0, (((1,), (1,)), ((), ())),
                         precision=lax.Precision.HIGHEST, preferred_element_type=F32)
    lane = lax.broadcasted_iota(jnp.int32, kr.shape, 1)
    rowi = lax.broadcasted_iota(jnp.int32, kr.shape, 0)
    dfull = jnp.concatenate([d_ref[0]] * ng, axis=1)
    kr_ref[...] = kr + jnp.where(lane == rowi * t, dfull, 0.0)

    causal = (lax.broadcasted_iota(jnp.int32, (t, t), 1)
              >= lax.broadcasted_iota(jnp.int32, (t, t), 0))

    def fill(n_in, carry):
        row = kr_ref[pl.ds(n_in, 1), :]
        r0 = pl.multiple_of(n_in * t, t)
        for n_out in range(ng):
            seg = jnp.broadcast_to(row[:, n_out * t:(n_out + 1) * t], (t, t))
            blk = pltpu.roll(seg, 0, 1, stride=1, stride_axis=0)
            toep_ref[pl.ds(r0, t), n_out * t:(n_out + 1) * t] = (
                jnp.where(causal, blk, 0.0).astype(BF16))
        return carry

    lax.fori_loop(0, ng, fill, 0)

    u = jnp.concatenate([u_ref[:, n, :] for n in range(ng)], axis=1).astype(BF16)
    xc_ref[...] = jnp.dot(u, gm, preferred_element_type=F32)
    xcs_ref[...] = jnp.dot(u, gms, preferred_element_type=F32)

    lam = lam_ref[0]
    a, b, c = lam[0:1], lam[1:2], lam[2:3]

    def body(ci, carry):
        st, st_sw = carry
        r = pl.multiple_of(ci * nb, nb)
        x0_ref[pl.ds(r, nb), :] = st
        return (st * a + st_sw * b + xc_ref[pl.ds(r, nb), :],
                st_sw * a + st * c + xcs_ref[pl.ds(r, nb), :])

    z = jnp.zeros((nb, 2 * SSM_STATE), F32)
    lax.fori_loop(0, nc, body, (z, z))

    x0 = x0_ref[...]
    x0_hi = x0.astype(BF16)
    x0_lo = (x0 - x0_hi.astype(F32)).astype(BF16)
    nt = (((1,), (1,)), ((), ()))
    y = jnp.dot(u, toep_ref[...], preferred_element_type=F32)
    y = y + lax.dot_general(x0_hi, hm, nt, preferred_element_type=F32)
    y = y + lax.dot_general(x0_lo, hm, nt, preferred_element_type=F32)
    for n_out in range(ng):
        y_ref[:, n_out, :] = y[:, n_out * t:(n_out + 1) * t]


def _ssm(ut, tables, *, nb, nc):
    rows, width, t = ut.shape
    ng = SSM_GROUP
    groups = width // ng
    st2 = 2 * SSM_STATE
    gw = ng * t
    grp_spec = pl.BlockSpec((rows, ng, t), lambda g: (0, g, 0))
    tab_specs = [pl.BlockSpec((1,) + tab.shape[1:], lambda g: (g, 0, 0)) for tab in tables]
    return pl.pallas_call(
        functools.partial(_ssm_kernel, nb=nb, nc=nc),
        grid=(groups,),
        in_specs=[grp_spec] + tab_specs,
        out_specs=grp_spec,
        out_shape=jax.ShapeDtypeStruct((rows, width, t), F32),
        scratch_shapes=[
            pltpu.VMEM((gw, gw), BF16),
            pltpu.VMEM((ng, gw), F32),
            pltpu.VMEM((rows, st2), F32),
            pltpu.VMEM((rows, st2), F32),
            pltpu.VMEM((rows, st2), F32),
        ],
        compiler_params=pltpu.CompilerParams(
            dimension_semantics=("parallel",), vmem_limit_bytes=V7X_VMEM_LIMIT_BYTES),
        name="s5_ssm",
    )(ut, *tables)


def _ssm_tables(a_re, a_im, log_dt, b_re, b_im, c_re, c_im, d, t):
    dt = jnp.exp(log_dt)[:, None]
    lre = a_re * dt
    ang = a_im * dt
    mag = jnp.exp(lre)
    lb_re, lb_im = mag * jnp.cos(ang), mag * jnp.sin(ang)
    den = a_re * a_re + a_im * a_im
    nr, ni = lb_re - 1.0, lb_im
    f_re = (nr * a_re + ni * a_im) / den
    f_im = (ni * a_re - nr * a_im) / den
    bb_re = f_re[..., None] * b_re - f_im[..., None] * b_im
    bb_im = f_re[..., None] * b_im + f_im[..., None] * b_re
    bt_re, bt_im = jnp.swapaxes(bb_re, 1, 2), jnp.swapaxes(bb_im, 1, 2)
    cat = lambda x, y: jnp.concatenate([x, y], axis=-1)
    k = jnp.arange(t + 1, dtype=F32)[None, :, None]
    pmag = jnp.exp(k * lre[:, None, :])
    p_re, p_im = pmag * jnp.cos(k * ang[:, None, :]), pmag * jnp.sin(k * ang[:, None, :])
    pa, pb = cat(p_re, p_re), cat(p_im, p_im)
    tr, ti = p_re[:, t], p_im[:, t]
    lam_t = jnp.stack([cat(tr, tr), cat(-ti, ti), cat(ti, -ti)], axis=1)
    lam_t = jnp.concatenate([lam_t, jnp.zeros((lam_t.shape[0], 5, lam_t.shape[2]), F32)], axis=1)
    g = a_re.shape[0]
    d_tab = jnp.broadcast_to(d.reshape(g, SSM_GROUP, 1), (g, SSM_GROUP, t))
    return (
        cat(c_re, -c_im), cat(-c_im, -c_re),
        cat(bt_re, bt_im), cat(-bt_im, bt_re),
        cat(bt_im, bt_re), cat(bt_re, -bt_im),
        cat(bt_re, bt_im),
        d_tab,
        pa[:, :t], pb[:, :t],
        pa[:, 1:], pb[:, 1:],
        pa[:, t - 1::-1][:, :t], pb[:, t - 1::-1][:, :t],
        lam_t,
    )


def _merge_mlp_kernel(x_ref, a_ref, yt_ref, gate_ref, wglu_ref, bglu_ref, wpa_ref, wps_ref,
                      wout_ref, gm_ref, wmi_ref, wmo_ref, o_ref, *, ff_tile):
    d = x_ref.shape[2]
    y = jnp.concatenate([yt_ref[j].T for j in range(yt_ref.shape[0])], axis=0)
    z = jax.nn.gelu(y, approximate=True)
    glu = jnp.dot(z.astype(BF16), wglu_ref[...], preferred_element_type=F32) + bglu_ref[...]
    s = z * jax.nn.sigmoid(glu)
    pa = jnp.dot(a_ref[0], wpa_ref[...], preferred_element_type=F32)
    ps = jnp.dot(s.astype(BF16), wps_ref[...], preferred_element_type=F32)
    merged = gate_ref[0, :, :d].astype(F32) * pa + gate_ref[0, :, d:].astype(F32) * ps
    x1 = x_ref[0] + jnp.dot(merged.astype(BF16), wout_ref[...], preferred_element_type=F32)
    ms = jnp.mean(x1 * x1, axis=-1, keepdims=True)
    hm = (x1 * lax.rsqrt(ms + EPS) * gm_ref[...]).astype(BF16)
    acc = x1
    for j in range(wmi_ref.shape[1] // ff_tile):
        hj = jnp.dot(hm, wmi_ref[:, j * ff_tile:(j + 1) * ff_tile], preferred_element_type=F32)
        hj = jnp.square(jnp.maximum(hj, 0.0)).astype(BF16)
        acc = acc + jnp.dot(hj, wmo_ref[j * ff_tile:(j + 1) * ff_tile, :],
                            preferred_element_type=F32)
    o_ref[0] = acc


def _merge_mlp(x, a, yt, gates, wglu, bglu, wpa, wps, wout, gmlp, wmi, wmo, *, tm, ff_tile):
    bsz, seq, d = x.shape
    tok = lambda b, i: (b, i, 0)
    consts = (wglu, bglu, wpa, wps, wout, gmlp, wmi, wmo)
    return pl.pallas_call(
        functools.partial(_merge_mlp_kernel, ff_tile=ff_tile),
        grid=(bsz, seq // tm),
        in_specs=[
            pl.BlockSpec((1, tm, d), tok), pl.BlockSpec((1, tm, a.shape[2]), tok),
            pl.BlockSpec((tm // SSM_T, None, yt.shape[2], SSM_T), lambda b, i: (i, b, 0, 0)),
            pl.BlockSpec((1, tm, gates.shape[2]), tok),
        ] + [_const_spec(c.shape) for c in consts],
        out_specs=pl.BlockSpec((1, tm, d), tok),
        out_shape=jax.ShapeDtypeStruct((bsz, seq, d), F32),
        compiler_params=pltpu.CompilerParams(
            dimension_semantics=("parallel", "parallel"), vmem_limit_bytes=V7X_VMEM_LIMIT_BYTES),
        name="merge_mlp",
    )(x, a, yt, gates, *consts)


def _lambda_init(layer_idx):
    return 0.8 - 0.6 * math.exp(-0.3 * layer_idx)


def _layer(x, layer_idx, norm_mix_g, w_in, b_gate, q_norm_g, k_norm_g, lambda_q1, lambda_k1,
           lambda_q2, lambda_k2, subln_g, ssm_a_re, ssm_a_im, ssm_log_dt, ssm_b_re, ssm_b_im,
           ssm_c_re, ssm_c_im, ssm_d, w_glu, b_glu, w_proj_attn, w_proj_ssm, w_out,
           norm_mlp_g, w_mlp_in, w_mlp_out):
    bsz, seq, d = x.shape
    qk_w = ATTN_HEADS * 2 * ATTN_QK_DIM
    v_w = ATTN_HEADS * ATTN_V_DIM
    u_w = ssm_d.shape[0]
    row = lambda v: v.reshape(1, -1).astype(F32)
    tm = min(TOKEN_TILE, seq)
    t = min(ATTN_TILE, seq)
    assert tm == t and seq % tm == 0 and tm % SSM_T == 0

    group_mean = jnp.kron(jnp.eye(qk_w // ATTN_QK_DIM, dtype=F32),
                          jnp.full((ATTN_QK_DIM, ATTN_QK_DIM), 1.0 / ATTN_QK_DIM, F32)).astype(BF16)
    reps = qk_w // ATTN_QK_DIM
    q_gain = row(jnp.tile(q_norm_g.astype(F32) * (ATTN_QK_DIM ** -0.5 * math.log2(math.e)), reps))
    k_gain = row(jnp.tile(k_norm_g.astype(F32), reps))
    q, k, vt, ut, gates = _in_proj(
        x, row(norm_mix_g), w_in.astype(BF16), group_mean, q_gain, k_gain, row(b_gate),
        qk_w=qk_w, v_w=v_w, u_w=u_w, tm=tm)

    lam_init = _lambda_init(layer_idx)
    lam = (jnp.exp(jnp.sum(lambda_q1.astype(F32) * lambda_k1.astype(F32)))
           - jnp.exp(jnp.sum(lambda_q2.astype(F32) * lambda_k2.astype(F32))) + lam_init)
    sub_gain = (subln_g.astype(F32) * (1.0 - lam_init)).reshape(-1, 1)
    score_bound = 1.02 * ATTN_QK_DIM * jnp.max(jnp.abs(q_gain)) * jnp.max(jnp.abs(k_gain))
    ctl = jnp.stack([lam, (score_bound <= ATTN_SCORE_LIMIT).astype(F32)])
    a = _attention(ctl, q, k, vt, sub_gain, t=t)

    nc = seq // SSM_T
    tables = _ssm_tables(
        ssm_a_re.astype(F32), ssm_a_im.astype(F32), ssm_log_dt.astype(F32),
        ssm_b_re.astype(F32), ssm_b_im.astype(F32), ssm_c_re.astype(F32), ssm_c_im.astype(F32),
        ssm_d.astype(F32), SSM_T)
    yt = _ssm(ut.reshape(nc * bsz, u_w, SSM_T), tables, nb=bsz, nc=nc)
    yt = yt.reshape(nc, bsz, u_w, SSM_T)

    return _merge_mlp(
        x, a, yt, gates, w_glu.astype(BF16), row(b_glu),
        w_proj_attn.astype(BF16), w_proj_ssm.astype(BF16), w_out.astype(BF16),
        row(norm_mlp_g), w_mlp_in.astype(BF16), w_mlp_out.astype(BF16),
        tm=tm, ff_tile=MLP_FF_TILE)


def kernel(x, norm_mix_g, w_in, b_gate, q_norm_g, k_norm_g, lambda_q1, lambda_k1, lambda_q2, lambda_k2, subln_g, ssm_a_re, ssm_a_im, ssm_log_dt, ssm_b_re, ssm_b_im, ssm_c_re, ssm_c_im, ssm_d, w_glu, b_glu, w_proj_attn, w_proj_ssm, w_out, norm_mlp_g, w_mlp_in, w_mlp_out):
    for l in range(norm_mix_g.shape[0]):
        x = _layer(
            x, l, norm_mix_g[l], w_in[l], b_gate[l], q_norm_g[l], k_norm_g[l],
            lambda_q1[l], lambda_k1[l], lambda_q2[l], lambda_k2[l], subln_g[l],
            ssm_a_re[l], ssm_a_im[l], ssm_log_dt[l], ssm_b_re[l], ssm_b_im[l],
            ssm_c_re[l], ssm_c_im[l], ssm_d[l], w_glu[l], b_glu[l],
            w_proj_attn[l], w_proj_ssm[l], w_out[l],
            norm_mlp_g[l], w_mlp_in[l], w_mlp_out[l])
    return x
```

```python
import functools
import math

import jax
import jax.numpy as jnp
from jax import lax
from jax.experimental import pallas as pl
from jax.experimental.pallas import tpu as pltpu

F32 = jnp.float32
BF16 = jnp.bfloat16

EPS = 1e-6
CHUNK = 64
ATTN_HEADS = 4
ATTN_QK_DIM = 64
ATTN_V_DIM = 2 * ATTN_QK_DIM
SSM_GROUP = 16
SSM_STATE = 64
LANES = 128
SSM_T = LANES
V7X_VMEM_LIMIT_BYTES = 56 * 1024 * 1024

TOKEN_TILE = 512
ATTN_TILE = 512
ATTN_SCORE_LIMIT = 60.0
MLP_FF_TILE = 1024


def _const_spec(shape):
    zeros = (0,) * len(shape)
    return pl.BlockSpec(shape, lambda *_: zeros, pipeline_mode=pl.Buffered(1))


def _in_proj_kernel(x_ref, g_ref, w_ref, bd_ref, qg_ref, kg_ref, bg_ref,
                    q_ref, k_ref, vt_ref, ut_ref, gate_ref, vu_ref, *, qk_w, v_w, u_w):
    x = x_ref[0]
    ms = jnp.mean(x * x, axis=-1, keepdims=True)
    h = (x * lax.rsqrt(ms + EPS) * g_ref[...]).astype(BF16)

    def proj(lo, hi):
        return jnp.dot(h, w_ref[:, lo:hi], preferred_element_type=F32)

    def group_norm(t, gain):
        ms_g = jnp.dot((t * t).astype(BF16), bd_ref[...], preferred_element_type=F32)
        return t * lax.rsqrt(ms_g + EPS) * gain

    o1 = qk_w
    o2 = o1 + qk_w
    o3 = o2 + v_w
    o4 = o3 + u_w
    q_ref[0] = group_norm(proj(0, o1), qg_ref[...]).astype(BF16)
    k_ref[0] = group_norm(proj(o1, o2), kg_ref[...]).astype(BF16)
    vu_ref[...] = proj(o2, o4)
    for hd in range(v_w // ATTN_V_DIM):
        vt_ref[0, hd, 0] = vu_ref[:, hd * ATTN_V_DIM:(hd + 1) * ATTN_V_DIM].T.astype(BF16)
    for j in range(ut_ref.shape[0]):
        ut_ref[j] = vu_ref[j * SSM_T:(j + 1) * SSM_T, v_w:].T
    gate_ref[0] = jax.nn.sigmoid(proj(o4, w_ref.shape[1]) + bg_ref[...]).astype(BF16)


def _in_proj(x, g, w, bd, qg, kg, bg, *, qk_w, v_w, u_w, tm):
    bsz, seq, d = x.shape
    gate_w = w.shape[1] - 2 * qk_w - v_w - u_w
    heads = v_w // ATTN_V_DIM
    tok = lambda b, i: (b, i, 0)
    out_shapes = (
        jax.ShapeDtypeStruct((bsz, seq, qk_w), BF16),
        jax.ShapeDtypeStruct((bsz, seq, qk_w), BF16),
        jax.ShapeDtypeStruct((bsz, heads, seq // tm, ATTN_V_DIM, tm), BF16),
        jax.ShapeDtypeStruct((seq // SSM_T, bsz, u_w, SSM_T), F32),
        jax.ShapeDtypeStruct((bsz, seq, gate_w), BF16),
    )
    return pl.pallas_call(
        functools.partial(_in_proj_kernel, qk_w=qk_w, v_w=v_w, u_w=u_w),
        grid=(bsz, seq // tm),
        in_specs=[
            pl.BlockSpec((1, tm, d), tok),
            _const_spec(g.shape), _const_spec(w.shape), _const_spec(bd.shape),
            _const_spec(qg.shape), _const_spec(kg.shape), _const_spec(bg.shape),
        ],
        out_specs=(
            pl.BlockSpec((1, tm, qk_w), tok), pl.BlockSpec((1, tm, qk_w), tok),
            pl.BlockSpec((1, heads, 1, ATTN_V_DIM, tm), lambda b, i: (b, 0, i, 0, 0)),
            pl.BlockSpec((tm // SSM_T, None, u_w, SSM_T), lambda b, i: (i, b, 0, 0)),
            pl.BlockSpec((1, tm, gate_w), tok),
        ),
        out_shape=out_shapes,
        scratch_shapes=[pltpu.VMEM((tm, v_w + u_w), F32)],
        compiler_params=pltpu.CompilerParams(
            dimension_semantics=("parallel", "parallel"), vmem_limit_bytes=V7X_VMEM_LIMIT_BYTES),
        name="in_proj",
    )(x, g, w, bd, qg, kg, bg)


def _attn_kernel(ctl_ref, q_ref, k_ref, vt_ref, sg_ref, o_ref,
                 qc_ref, m_ref, l_ref, al_ref, acc_ref, l8_ref, s_ref, p_ref, *, t):
    i = pl.program_id(2)
    q = q_ref[0]
    lane = lax.broadcasted_iota(jnp.int32, q.shape, 1)
    zero = jnp.zeros_like(q)
    qc_ref[...] = jnp.concatenate([jnp.where(lane < ATTN_QK_DIM, q, zero),
                                   jnp.where(lane >= ATTN_QK_DIM, q, zero)], axis=0)
    acc_ref[...] = jnp.zeros(acc_ref.shape, F32)
    bounded = ctl_ref[1] > 0.0

    def raw_scores(j, masked):
        kt = k_ref[0, pl.ds(pl.multiple_of(j * t, t), t), :]
        s = lax.dot_general(kt, qc_ref[...], (((1,), (1,)), ((), ())),
                            preferred_element_type=F32)
        if masked:
            key_chunk = lax.broadcasted_iota(jnp.int32, s.shape, 0) // CHUNK
            qry = lax.broadcasted_iota(jnp.int32, s.shape, 1)
            qry_chunk = jnp.where(qry >= t, qry - t, qry) // CHUNK
            s = jnp.where(key_chunk <= qry_chunk, s, -jnp.inf)
        return s

    def pipeline(first, step, finish):
        @pl.when(i > 0)
        def _():
            first(0, False, 0)

            def body(h, carry):
                step(2 * h, 0, False)
                step(2 * h + 1, 1, False)
                return carry

            lax.fori_loop(0, (i - 1) // 2, body, 0)

        @pl.when(i % 2 == 1)
        def _():
            step(i - 1, 0, True)
            finish(i, 1)

        @pl.when(jnp.logical_and(i % 2 == 0, i > 0))
        def _():
            step(i - 2, 0, False)
            step(i - 1, 1, True)
            finish(i, 0)

        @pl.when(i == 0)
        def _():
            first(0, True, 0)
            finish(0, 0)

    def probs(j, masked, slot):
        p = jnp.exp2(raw_scores(j, masked))
        l8_ref[...] += jnp.sum(p.reshape(t // 8, 8, 2 * t), axis=0)
        p_ref[slot] = p.astype(BF16)

    def accumulate(j, slot):
        acc_ref[...] += jnp.dot(vt_ref[0, 0, j], p_ref[slot], preferred_element_type=F32)

    def bounded_step(j, slot, masked_next):
        accumulate(j, slot)
        probs(j + 1, masked_next, 1 - slot)

    @pl.when(bounded)
    def _():
        l8_ref[...] = jnp.zeros(l8_ref.shape, F32)
        pipeline(probs, bounded_step, accumulate)
        l_ref[...] = jnp.sum(l8_ref[...], axis=0, keepdims=True)

    def scores(j, masked, slot):
        s_ref[slot] = raw_scores(j, masked)

    def softmax(slot):
        s = s_ref[slot]
        m_old = m_ref[...]
        m_new = jnp.maximum(m_old, jnp.max(s, axis=0, keepdims=True))
        alpha = jnp.exp2(m_old - m_new)
        p = jnp.exp2(s - m_new)
        l_ref[...] = alpha * l_ref[...] + jnp.sum(p, axis=0, keepdims=True)
        m_ref[...] = m_new
        al_ref[...] = alpha
        p_ref[slot] = p.astype(BF16)

    def values(j, slot):
        acc_ref[...] = al_ref[...] * acc_ref[...] + jnp.dot(
            vt_ref[0, 0, jnp.maximum(j, 0)], p_ref[slot],
            preferred_element_type=F32)

    def online_step(j, slot, masked_next):
        values(j - 1, 1 - slot)
        scores(j + 1, masked_next, 1 - slot)
        softmax(slot)

    def online_finish(j, slot):
        values(j - 1, 1 - slot)
        softmax(slot)
        values(j, slot)

    @pl.when(jnp.logical_not(bounded))
    def _():
        m_ref[...] = jnp.full(m_ref.shape, -jnp.inf, F32)
        l_ref[...] = jnp.zeros(l_ref.shape, F32)
        al_ref[...] = jnp.ones(al_ref.shape, F32)
        p_ref[1] = jnp.zeros(p_ref.shape[1:], BF16)
        pipeline(scores, online_step, online_finish)

    o = acc_ref[...] * (1.0 / l_ref[...])
    o = o[:, :t] - ctl_ref[0] * o[:, t:]
    ms = jnp.mean(o * o, axis=0, keepdims=True)
    o = o * lax.rsqrt(ms + EPS) * sg_ref[...]
    o_ref[0] = o.T.astype(BF16)


def _attention(ctl, q, k, vt, sg, *, t):
    b, s, w = q.shape
    heads = w // ATTN_V_DIM
    nk = s // t
    return pl.pallas_call(
        functools.partial(_attn_kernel, t=t),
        grid=(b, heads, nk),
        in_specs=[
            pl.BlockSpec(memory_space=pltpu.SMEM),
            pl.BlockSpec((1, t, ATTN_V_DIM), lambda bi, hi, i: (bi, i, hi)),
            pl.BlockSpec((1, s, ATTN_V_DIM), lambda bi, hi, i: (bi, 0, hi)),
            pl.BlockSpec((1, 1, nk, ATTN_V_DIM, t), lambda bi, hi, i: (bi, hi, 0, 0, 0)),
            _const_spec(sg.shape),
        ],
        out_specs=pl.BlockSpec((1, t, ATTN_V_DIM), lambda bi, hi, i: (bi, i, hi)),
        out_shape=jax.ShapeDtypeStruct((b, s, w), BF16),
        scratch_shapes=[
            pltpu.VMEM((2 * t, ATTN_V_DIM), BF16),
            pltpu.VMEM((1, 2 * t), F32),
            pltpu.VMEM((1, 2 * t), F32),
            pltpu.VMEM((1, 2 * t), F32),
            pltpu.VMEM((ATTN_V_DIM, 2 * t), F32),
            pltpu.VMEM((8, 2 * t), F32),
            pltpu.VMEM((2, t, 2 * t), F32),
            pltpu.VMEM((2, t, 2 * t), BF16),
        ],
        compiler_params=pltpu.CompilerParams(
            dimension_semantics=("parallel", "parallel", "arbitrary"),
            vmem_limit_bytes=V7X_VMEM_LIMIT_BYTES),
        name="diff_attn",
    )(ctl, q, k, vt, sg)


def _ssm_kernel(*refs, nb, nc):
    ng = SSM_GROUP
    (u_ref, ca_ref, cb_ref, ba_ref, bb_ref, bas_ref, bbs_ref, b2_ref, d_ref,
     pa0_ref, pb0_ref, pa1_ref, pb1_ref, qa_ref, qb_ref, lam_ref,
     y_ref, toep_ref, kr_ref, xc_ref, xcs_ref, x0_ref) = refs
    t = SSM_T
    gw = ng * t

    def rep(ref):
        v = ref[0]
        return jnp.broadcast_to(v[:, None, :], (ng, t, v.shape[-1])).reshape(gw, v.shape[-1])

    def tile(ref):
        v = ref[0]
        return jnp.broadcast_to(v[None], (ng, t, v.shape[-1])).reshape(gw, v.shape[-1])

    rca, rcb = rep(ca_ref), rep(cb_ref)
    cl0 = rca * tile(pa0_ref) + rcb * tile(pb0_ref)
    hm = (rca * tile(pa1_ref) + rcb * tile(pb1_ref)).astype(BF16)
    tqa, tqb = tile(qa_ref), tile(qb_ref)
    gm = (rep(ba_ref) * tqa + rep(bb_ref) * tqb).astype(BF16)
    gms = (rep(bas_ref) * tqa + rep(bbs_ref) * tqb).astype(BF16)

    kr = lax.dot_general(b2_ref[0], cl0, (((1,), (1,)), ((), ())),
                         precision=lax.Precision.HIGHEST, preferred_element_type=F32)
    lane = lax.broadcasted_iota(jnp.int32, kr.shape, 1)
    rowi = lax.broadcasted_iota(jnp.int32, kr.shape, 0)
    dfull = jnp.concatenate([d_ref[0]] * ng, axis=1)
    kr_ref[...] = kr + jnp.where(lane == rowi * t, dfull, 0.0)

    causal = (lax.broadcasted_iota(jnp.int32, (t, t), 1)
              >= lax.broadcasted_iota(jnp.int32, (t, t), 0))

    def fill(n_in, carry):
        row = kr_ref[pl.ds(n_in, 1), :]
        r0 = pl.multiple_of(n_in * t, t)
        for n_out in range(ng):
            seg = jnp.broadcast_to(row[:, n_out * t:(n_out + 1) * t], (t, t))
            blk = pltpu.roll(seg, 0, 1, stride=1, stride_axis=0)
            toep_ref[pl.ds(r0, t), n_out * t:(n_out + 1) * t] = (
                jnp.where(causal, blk, 0.0).astype(BF16))
        return carry

    lax.fori_loop(0, ng, fill, 0)

    u = jnp.concatenate([u_ref[:, n, :] for n in range(ng)], axis=1).astype(BF16)
    xcc = jnp.dot(u, jnp.concatenate([gm, gms], axis=1), preferred_element_type=F32)
    xc_ref[...] = xcc[:, :2 * SSM_STATE]
    xcs_ref[...] = xcc[:, 2 * SSM_STATE:]

    lam = lam_ref[0]
    a, b, c = lam[0:1], lam[1:2], lam[2:3]

    def body(ci, carry):
        st, st_sw = carry
        r = pl.multiple_of(ci * nb, nb)
        x0_ref[pl.ds(r, nb), :] = st
        return (st * a + st_sw * b + xc_ref[pl.ds(r, nb), :],
                st_sw * a + st * c + xcs_ref[pl.ds(r, nb), :])

    z = jnp.zeros((nb, 2 * SSM_STATE), F32)
    lax.fori_loop(0, nc, body, (z, z))

    x0 = x0_ref[...]
    x0_hi = x0.astype(BF16)
    x0_lo = (x0 - x0_hi.astype(F32)).astype(BF16)
    nt = (((1,), (1,)), ((), ()))
    y = jnp.dot(u, toep_ref[...], preferred_element_type=F32)
    y = y + lax.dot_general(jnp.concatenate([x0_hi, x0_lo], axis=1),
                            jnp.concatenate([hm, hm], axis=1), nt, preferred_element_type=F32)
    for n_out in range(ng):
        y_ref[:, n_out, :] = y[:, n_out * t:(n_out + 1) * t]


def _ssm(ut, tables, *, nb, nc):
    rows, width, t = ut.shape
    ng = SSM_GROUP
    groups = width // ng
    st2 = 2 * SSM_STATE
    gw = ng * t
    grp_spec = pl.BlockSpec((rows, ng, t), lambda g: (0, g, 0))
    tab_specs = [pl.BlockSpec((1,) + tab.shape[1:], lambda g: (g, 0, 0)) for tab in tables]
    return pl.pallas_call(
        functools.partial(_ssm_kernel, nb=nb, nc=nc),
        grid=(groups,),
        in_specs=[grp_spec] + tab_specs,
        out_specs=grp_spec,
        out_shape=jax.ShapeDtypeStruct((rows, width, t), F32),
        scratch_shapes=[
            pltpu.VMEM((gw, gw), BF16),
            pltpu.VMEM((ng, gw), F32),
            pltpu.VMEM((rows, st2), F32),
            pltpu.VMEM((rows, st2), F32),
            pltpu.VMEM((rows, st2), F32),
        ],
        compiler_params=pltpu.CompilerParams(
            dimension_semantics=("parallel",), vmem_limit_bytes=V7X_VMEM_LIMIT_BYTES),
        name="s5_ssm",
    )(ut, *tables)


def _ssm_tables(a_re, a_im, log_dt, b_re, b_im, c_re, c_im, d, t):
    dt = jnp.exp(log_dt)[:, None]
    lre = a_re * dt
    ang = a_im * dt
    mag = jnp.exp(lre)
    lb_re, lb_im = mag * jnp.cos(ang), mag * jnp.sin(ang)
    den = a_re * a_re + a_im * a_im
    nr, ni = lb_re - 1.0, lb_im
    f_re = (nr * a_re + ni * a_im) / den
    f_im = (ni * a_re - nr * a_im) / den
    bb_re = f_re[..., None] * b_re - f_im[..., None] * b_im
    bb_im = f_re[..., None] * b_im + f_im[..., None] * b_re
    bt_re, bt_im = jnp.swapaxes(bb_re, 1, 2), jnp.swapaxes(bb_im, 1, 2)
    cat = lambda x, y: jnp.concatenate([x, y], axis=-1)
    k = jnp.arange(t + 1, dtype=F32)[None, :, None]
    pmag = jnp.exp(k * lre[:, None, :])
    p_re, p_im = pmag * jnp.cos(k * ang[:, None, :]), pmag * jnp.sin(k * ang[:, None, :])
    pa, pb = cat(p_re, p_re), cat(p_im, p_im)
    tr, ti = p_re[:, t], p_im[:, t]
    lam_t = jnp.stack([cat(tr, tr), cat(-ti, ti), cat(ti, -ti)], axis=1)
    lam_t = jnp.concatenate([lam_t, jnp.zeros((lam_t.shape[0], 5, lam_t.shape[2]), F32)], axis=1)
    g = a_re.shape[0]
    d_tab = jnp.broadcast_to(d.reshape(g, SSM_GROUP, 1), (g, SSM_GROUP, t))
    return (
        cat(c_re, -c_im), cat(-c_im, -c_re),
        cat(bt_re, bt_im), cat(-bt_im, bt_re),
        cat(bt_im, bt_re), cat(bt_re, -bt_im),
        cat(bt_re, bt_im),
        d_tab,
        pa[:, :t], pb[:, :t],
        pa[:, 1:], pb[:, 1:],
        pa[:, t - 1::-1][:, :t], pb[:, t - 1::-1][:, :t],
        lam_t,
    )


def _merge_mlp_kernel(x_ref, a_ref, yt_ref, gate_ref, wglu_ref, bglu_ref, wpa_ref, wps_ref,
                      wout_ref, gm_ref, wmi_ref, wmo_ref, o_ref, *, ff_tile):
    d = x_ref.shape[2]
    y = jnp.concatenate([yt_ref[j].T for j in range(yt_ref.shape[0])], axis=0)
    z = jax.nn.gelu(y, approximate=True)
    glu = jnp.dot(z.astype(BF16), wglu_ref[...], preferred_element_type=F32) + bglu_ref[...]
    s = z * jax.nn.sigmoid(glu)
    pa = jnp.dot(a_ref[0], wpa_ref[...], preferred_element_type=F32)
    ps = jnp.dot(s.astype(BF16), wps_ref[...], preferred_element_type=F32)
    merged = gate_ref[0, :, :d].astype(F32) * pa + gate_ref[0, :, d:].astype(F32) * ps
    x1 = x_ref[0] + jnp.dot(merged.astype(BF16), wout_ref[...], preferred_element_type=F32)
    ms = jnp.mean(x1 * x1, axis=-1, keepdims=True)
    hm = (x1 * lax.rsqrt(ms + EPS) * gm_ref[...]).astype(BF16)
    acc = x1
    for j in range(wmi_ref.shape[1] // ff_tile):
        hj = jnp.dot(hm, wmi_ref[:, j * ff_tile:(j + 1) * ff_tile], preferred_element_type=F32)
        hj = jnp.square(jnp.maximum(hj, 0.0)).astype(BF16)
        acc = acc + jnp.dot(hj, wmo_ref[j * ff_tile:(j + 1) * ff_tile, :],
                            preferred_element_type=F32)
    o_ref[0] = acc


def _merge_mlp(x, a, yt, gates, wglu, bglu, wpa, wps, wout, gmlp, wmi, wmo, *, tm, ff_tile):
    bsz, seq, d = x.shape
    tok = lambda b, i: (b, i, 0)
    consts = (wglu, bglu, wpa, wps, wout, gmlp, wmi, wmo)
    return pl.pallas_call(
        functools.partial(_merge_mlp_kernel, ff_tile=ff_tile),
        grid=(bsz, seq // tm),
        in_specs=[
            pl.BlockSpec((1, tm, d), tok), pl.BlockSpec((1, tm, a.shape[2]), tok),
            pl.BlockSpec((tm // SSM_T, None, yt.shape[2], SSM_T), lambda b, i: (i, b, 0, 0)),
            pl.BlockSpec((1, tm, gates.shape[2]), tok),
        ] + [_const_spec(c.shape) for c in consts],
        out_specs=pl.BlockSpec((1, tm, d), tok),
        out_shape=jax.ShapeDtypeStruct((bsz, seq, d), F32),
        compiler_params=pltpu.CompilerParams(
            dimension_semantics=("parallel", "parallel"), vmem_limit_bytes=V7X_VMEM_LIMIT_BYTES),
        name="merge_mlp",
    )(x, a, yt, gates, *consts)


def _lambda_init(layer_idx):
    return 0.8 - 0.6 * math.exp(-0.3 * layer_idx)


def _layer(x, layer_idx, norm_mix_g, w_in, b_gate, q_norm_g, k_norm_g, lambda_q1, lambda_k1,
           lambda_q2, lambda_k2, subln_g, ssm_a_re, ssm_a_im, ssm_log_dt, ssm_b_re, ssm_b_im,
           ssm_c_re, ssm_c_im, ssm_d, w_glu, b_glu, w_proj_attn, w_proj_ssm, w_out,
           norm_mlp_g, w_mlp_in, w_mlp_out):
    bsz, seq, d = x.shape
    qk_w = ATTN_HEADS * 2 * ATTN_QK_DIM
    v_w = ATTN_HEADS * ATTN_V_DIM
    u_w = ssm_d.shape[0]
    row = lambda v: v.reshape(1, -1).astype(F32)
    tm = min(TOKEN_TILE, seq)
    t = min(ATTN_TILE, seq)
    assert tm == t and seq % tm == 0 and tm % SSM_T == 0

    group_mean = jnp.kron(jnp.eye(qk_w // ATTN_QK_DIM, dtype=F32),
                          jnp.full((ATTN_QK_DIM, ATTN_QK_DIM), 1.0 / ATTN_QK_DIM, F32)).astype(BF16)
    reps = qk_w // ATTN_QK_DIM
    q_gain = row(jnp.tile(q_norm_g.astype(F32) * (ATTN_QK_DIM ** -0.5 * math.log2(math.e)), reps))
    k_gain = row(jnp.tile(k_norm_g.astype(F32), reps))
    q, k, vt, ut, gates = _in_proj(
        x, row(norm_mix_g), w_in.astype(BF16), group_mean, q_gain, k_gain, row(b_gate),
        qk_w=qk_w, v_w=v_w, u_w=u_w, tm=tm)

    lam_init = _lambda_init(layer_idx)
    lam = (jnp.exp(jnp.sum(lambda_q1.astype(F32) * lambda_k1.astype(F32)))
           - jnp.exp(jnp.sum(lambda_q2.astype(F32) * lambda_k2.astype(F32))) + lam_init)
    sub_gain = (subln_g.astype(F32) * (1.0 - lam_init)).reshape(-1, 1)
    score_bound = 1.02 * ATTN_QK_DIM * jnp.max(jnp.abs(q_gain)) * jnp.max(jnp.abs(k_gain))
    ctl = jnp.stack([lam, (score_bound <= ATTN_SCORE_LIMIT).astype(F32)])
    a = _attention(ctl, q, k, vt, sub_gain, t=t)

    nc = seq // SSM_T
    tables = _ssm_tables(
        ssm_a_re.astype(F32), ssm_a_im.astype(F32), ssm_log_dt.astype(F32),
        ssm_b_re.astype(F32), ssm_b_im.astype(F32), ssm_c_re.astype(F32), ssm_c_im.astype(F32),
        ssm_d.astype(F32), SSM_T)
    yt = _ssm(ut.reshape(nc * bsz, u_w, SSM_T), tables, nb=bsz, nc=nc)
    yt = yt.reshape(nc, bsz, u_w, SSM_T)

    return _merge_mlp(
        x, a, yt, gates, w_glu.astype(BF16), row(b_glu),
        w_proj_attn.astype(BF16), w_proj_ssm.astype(BF16), w_out.astype(BF16),
        row(norm_mlp_g), w_mlp_in.astype(BF16), w_mlp_out.astype(BF16),
        tm=tm, ff_tile=MLP_FF_TILE)


def kernel(x, norm_mix_g, w_in, b_gate, q_norm_g, k_norm_g, lambda_q1, lambda_k1, lambda_q2, lambda_k2, subln_g, ssm_a_re, ssm_a_im, ssm_log_dt, ssm_b_re, ssm_b_im, ssm_c_re, ssm_c_im, ssm_d, w_glu, b_glu, w_proj_attn, w_proj_ssm, w_out, norm_mlp_g, w_mlp_in, w_mlp_out):
    for l in range(norm_mix_g.shape[0]):
        x = _layer(
            x, l, norm_mix_g[l], w_in[l], b_gate[l], q_norm_g[l], k_norm_g[l],
            lambda_q1[l], lambda_k1[l], lambda_q2[l], lambda_k2[l], subln_g[l],
            ssm_a_re[l], ssm_a_im[l], ssm_log_dt[l], ssm_b_re[l], ssm_b_im[l],
            ssm_c_re[l], ssm_c_im[l], ssm_d[l], w_glu[l], b_glu[l],
            w_proj_attn[l], w_proj_ssm[l], w_out[l],
            norm_mlp_g[l], w_mlp_in[l], w_mlp_out[l])
    return x
```

```python
import functools
import math

import jax
import jax.numpy as jnp
from jax import lax
from jax.experimental import pallas as pl
from jax.experimental.pallas import tpu as pltpu

F32 = jnp.float32
BF16 = jnp.bfloat16

EPS = 1e-6
CHUNK = 64
ATTN_HEADS = 4
ATTN_QK_DIM = 64
ATTN_V_DIM = 2 * ATTN_QK_DIM
SSM_GROUP = 16
SSM_STATE = 64
LANES = 128
SSM_T = LANES
V7X_VMEM_LIMIT_BYTES = 56 * 1024 * 1024

TOKEN_TILE = 512
ATTN_TILE = 512
ATTN_SCORE_LIMIT = 60.0
MLP_FF_TILE = 1024


def _const_spec(shape):
    zeros = (0,) * len(shape)
    return pl.BlockSpec(shape, lambda *_: zeros, pipeline_mode=pl.Buffered(1))


def _in_proj_kernel(x_ref, g_ref, w_ref, bd_ref, qg_ref, kg_ref, bg_ref,
                    q_ref, k_ref, vt_ref, ut_ref, gate_ref, vu_ref, *, qk_w, v_w, u_w):
    x = x_ref[0]
    ms = jnp.mean(x * x, axis=-1, keepdims=True)
    h = (x * lax.rsqrt(ms + EPS) * g_ref[...]).astype(BF16)

    def proj(lo, hi):
        return jnp.dot(h, w_ref[:, lo:hi], preferred_element_type=F32)

    def group_norm(t, gain):
        ms_g = jnp.dot((t * t).astype(BF16), bd_ref[...], preferred_element_type=F32)
        return t * lax.rsqrt(ms_g + EPS) * gain

    o1 = qk_w
    o2 = o1 + qk_w
    o3 = o2 + v_w
    o4 = o3 + u_w
    q_ref[0] = group_norm(proj(0, o1), qg_ref[...]).astype(BF16)
    k_ref[0] = group_norm(proj(o1, o2), kg_ref[...]).astype(BF16)
    vu_ref[...] = proj(o2, o4)
    for hd in range(v_w // ATTN_V_DIM):
        vt_ref[0, hd, 0] = vu_ref[:, hd * ATTN_V_DIM:(hd + 1) * ATTN_V_DIM].T.astype(BF16)
    for j in range(ut_ref.shape[0]):
        ut_ref[j] = vu_ref[j * SSM_T:(j + 1) * SSM_T, v_w:].T
    gate_ref[0] = jax.nn.sigmoid(proj(o4, w_ref.shape[1]) + bg_ref[...]).astype(BF16)


def _in_proj(x, g, w, bd, qg, kg, bg, *, qk_w, v_w, u_w, tm):
    bsz, seq, d = x.shape
    gate_w = w.shape[1] - 2 * qk_w - v_w - u_w
    heads = v_w // ATTN_V_DIM
    tok = lambda b, i: (b, i, 0)
    out_shapes = (
        jax.ShapeDtypeStruct((bsz, seq, qk_w), BF16),
        jax.ShapeDtypeStruct((bsz, seq, qk_w), BF16),
        jax.ShapeDtypeStruct((bsz, heads, seq // tm, ATTN_V_DIM, tm), BF16),
        jax.ShapeDtypeStruct((seq // SSM_T, bsz, u_w, SSM_T), F32),
        jax.ShapeDtypeStruct((bsz, seq, gate_w), BF16),
    )
    return pl.pallas_call(
        functools.partial(_in_proj_kernel, qk_w=qk_w, v_w=v_w, u_w=u_w),
        grid=(bsz, seq // tm),
        in_specs=[
            pl.BlockSpec((1, tm, d), tok),
            _const_spec(g.shape), _const_spec(w.shape), _const_spec(bd.shape),
            _const_spec(qg.shape), _const_spec(kg.shape), _const_spec(bg.shape),
        ],
        out_specs=(
            pl.BlockSpec((1, tm, qk_w), tok), pl.BlockSpec((1, tm, qk_w), tok),
            pl.BlockSpec((1, heads, 1, ATTN_V_DIM, tm), lambda b, i: (b, 0, i, 0, 0)),
            pl.BlockSpec((tm // SSM_T, None, u_w, SSM_T), lambda b, i: (i, b, 0, 0)),
            pl.BlockSpec((1, tm, gate_w), tok),
        ),
        out_shape=out_shapes,
        scratch_shapes=[pltpu.VMEM((tm, v_w + u_w), F32)],
        compiler_params=pltpu.CompilerParams(
            dimension_semantics=("parallel", "parallel"), vmem_limit_bytes=V7X_VMEM_LIMIT_BYTES),
        name="in_proj",
    )(x, g, w, bd, qg, kg, bg)


def _attn_kernel(ctl_ref, q_ref, k_ref, vt_ref, sg_ref, o_ref,
                 qc_ref, m_ref, l_ref, al_ref, acc_ref, l8_ref, s_ref, p_ref, *, t):
    i = pl.program_id(2)
    q = q_ref[0]
    lane = lax.broadcasted_iota(jnp.int32, q.shape, 1)
    zero = jnp.zeros_like(q)
    qc_ref[...] = jnp.concatenate([jnp.where(lane < ATTN_QK_DIM, q, zero),
                                   jnp.where(lane >= ATTN_QK_DIM, q, zero)], axis=0)
    acc_ref[...] = jnp.zeros(acc_ref.shape, F32)
    bounded = ctl_ref[1] > 0.0

    def raw_scores(j, masked):
        kt = k_ref[0, pl.ds(pl.multiple_of(j * t, t), t), :]
        s = lax.dot_general(kt, qc_ref[...], (((1,), (1,)), ((), ())),
                            preferred_element_type=F32)
        if masked:
            key_chunk = lax.broadcasted_iota(jnp.int32, s.shape, 0) // CHUNK
            qry = lax.broadcasted_iota(jnp.int32, s.shape, 1)
            qry_chunk = jnp.where(qry >= t, qry - t, qry) // CHUNK
            s = jnp.where(key_chunk <= qry_chunk, s, -jnp.inf)
        return s

    def pipeline(first, step, finish):
        @pl.when(i > 0)
        def _():
            first(0, False, 0)

            def body(h, carry):
                step(2 * h, 0, False)
                step(2 * h + 1, 1, False)
                return carry

            lax.fori_loop(0, (i - 1) // 2, body, 0)

        @pl.when(i % 2 == 1)
        def _():
            step(i - 1, 0, True)
            finish(i, 1)

        @pl.when(jnp.logical_and(i % 2 == 0, i > 0))
        def _():
            step(i - 2, 0, False)
            step(i - 1, 1, True)
            finish(i, 0)

        @pl.when(i == 0)
        def _():
            first(0, True, 0)
            finish(0, 0)

    def probs(j, masked, slot):
        p = jnp.exp2(raw_scores(j, masked))
        l8_ref[...] += jnp.sum(p.reshape(t // 8, 8, 2 * t), axis=0)
        p_ref[slot] = p.astype(BF16)

    def accumulate(j, slot):
        acc_ref[...] += jnp.dot(vt_ref[0, 0, j], p_ref[slot], preferred_element_type=F32)

    def bounded_step(j, slot, masked_next):
        accumulate(j, slot)
        probs(j + 1, masked_next, 1 - slot)

    @pl.when(bounded)
    def _():
        l8_ref[...] = jnp.zeros(l8_ref.shape, F32)
        pipeline(probs, bounded_step, accumulate)
        l_ref[...] = jnp.sum(l8_ref[...], axis=0, keepdims=True)

    def scores(j, masked, slot):
        s_ref[slot] = raw_scores(j, masked)

    def softmax(slot):
        s = s_ref[slot]
        m_old = m_ref[...]
        m_new = jnp.maximum(m_old, jnp.max(s, axis=0, keepdims=True))
        alpha = jnp.exp2(m_old - m_new)
        p = jnp.exp2(s - m_new)
        l_ref[...] = alpha * l_ref[...] + jnp.sum(p, axis=0, keepdims=True)
        m_ref[...] = m_new
        al_ref[...] = alpha
        p_ref[slot] = p.astype(BF16)

    def values(j, slot):
        acc_ref[...] = al_ref[...] * acc_ref[...] + jnp.dot(
            vt_ref[0, 0, jnp.maximum(j, 0)], p_ref[slot],
            preferred_element_type=F32)

    def online_step(j, slot, masked_next):
        values(j - 1, 1 - slot)
        scores(j + 1, masked_next, 1 - slot)
        softmax(slot)

    def online_finish(j, slot):
        values(j - 1, 1 - slot)
        softmax(slot)
        values(j, slot)

    @pl.when(jnp.logical_not(bounded))
    def _():
        m_ref[...] = jnp.full(m_ref.shape, -jnp.inf, F32)
        l_ref[...] = jnp.zeros(l_ref.shape, F32)
        al_ref[...] = jnp.ones(al_ref.shape, F32)
        p_ref[1] = jnp.zeros(p_ref.shape[1:], BF16)
        pipeline(scores, online_step, online_finish)

    o = acc_ref[...] * (1.0 / l_ref[...])
    o = o[:, :t] - ctl_ref[0] * o[:, t:]
    ms = jnp.mean(o * o, axis=0, keepdims=True)
    o = o * lax.rsqrt(ms + EPS) * sg_ref[...]
    o_ref[0] = o.T.astype(BF16)


def _attention(ctl, q, k, vt, sg, *, t):
    b, s, w = q.shape
    heads = w // ATTN_V_DIM
    nk = s // t
    return pl.pallas_call(
        functools.partial(_attn_kernel, t=t),
        grid=(b, heads, nk),
        in_specs=[
            pl.BlockSpec(memory_space=pltpu.SMEM),
            pl.BlockSpec((1, t, ATTN_V_DIM), lambda bi, hi, i: (bi, i, hi)),
            pl.BlockSpec((1, s, ATTN_V_DIM), lambda bi, hi, i: (bi, 0, hi)),
            pl.BlockSpec((1, 1, nk, ATTN_V_DIM, t), lambda bi, hi, i: (bi, hi, 0, 0, 0)),
            _const_spec(sg.shape),
        ],
        out_specs=pl.BlockSpec((1, t, ATTN_V_DIM), lambda bi, hi, i: (bi, i, hi)),
        out_shape=jax.ShapeDtypeStruct((b, s, w), BF16),
        scratch_shapes=[
            pltpu.VMEM((2 * t, ATTN_V_DIM), BF16),
            pltpu.VMEM((1, 2 * t), F32),
            pltpu.VMEM((1, 2 * t), F32),
            pltpu.VMEM((1, 2 * t), F32),
            pltpu.VMEM((ATTN_V_DIM, 2 * t), F32),
            pltpu.VMEM((8, 2 * t), F32),
            pltpu.VMEM((2, t, 2 * t), F32),
            pltpu.VMEM((2, t, 2 * t), BF16),
        ],
        compiler_params=pltpu.CompilerParams(
            dimension_semantics=("parallel", "parallel", "arbitrary"),
            vmem_limit_bytes=V7X_VMEM_LIMIT_BYTES),
        name="diff_attn",
    )(ctl, q, k, vt, sg)


def _ssm_kernel(*refs, nb, nc):
    ng = SSM_GROUP
    (u_ref, ca_ref, cb_ref, ba_ref, bb_ref, bas_ref, bbs_ref, b2_ref, d_ref,
     pa0_ref, pb0_ref, pa1_ref, pb1_ref, qa_ref, qb_ref, lam_ref,
     y_ref, toep_ref, xc_ref, xcs_ref, x0_ref) = refs
    t = SSM_T
    gw = ng * t

    def rep(ref):
        v = ref[0]
        return jnp.broadcast_to(v[:, None, :], (ng, t, v.shape[-1])).reshape(gw, v.shape[-1])

    def tile(ref):
        v = ref[0]
        return jnp.broadcast_to(v[None], (ng, t, v.shape[-1])).reshape(gw, v.shape[-1])

    rca, rcb = rep(ca_ref), rep(cb_ref)
    cl0 = rca * tile(pa0_ref) + rcb * tile(pb0_ref)
    hm = (rca * tile(pa1_ref) + rcb * tile(pb1_ref)).astype(BF16)
    tqa, tqb = tile(qa_ref), tile(qb_ref)
    gm = (rep(ba_ref) * tqa + rep(bb_ref) * tqb).astype(BF16)
    gms = (rep(bas_ref) * tqa + rep(bbs_ref) * tqb).astype(BF16)

    kr = lax.dot_general(b2_ref[0], cl0, (((1,), (1,)), ((), ())),
                         precision=lax.Precision.HIGHEST, preferred_element_type=F32)
    lane = lax.broadcasted_iota(jnp.int32, kr.shape, 1)
    rowi = lax.broadcasted_iota(jnp.int32, kr.shape, 0)
    dfull = jnp.concatenate([d_ref[0]] * ng, axis=1)
    kr = kr + jnp.where(lane == rowi * t, dfull, 0.0)

    u = jnp.concatenate([u_ref[:, n, :] for n in range(ng)], axis=1).astype(BF16)

    causal = (lax.broadcasted_iota(jnp.int32, (t, t), 1)
              >= lax.broadcasted_iota(jnp.int32, (t, t), 0))
    y = None
    for pair in range(ng // 2):
        for n_in in (2 * pair, 2 * pair + 1):
            for n_out in range(ng):
                seg = jnp.broadcast_to(kr[n_in:n_in + 1, n_out * t:(n_out + 1) * t], (t, t))
                blk = pltpu.roll(seg, 0, 1, stride=1, stride_axis=0)
                toep_ref[n_in * t:(n_in + 1) * t, n_out * t:(n_out + 1) * t] = (
                    jnp.where(causal, blk, 0.0).astype(BF16))
        lo, hi = 2 * pair * t, (2 * pair + 2) * t
        part = jnp.dot(u[:, lo:hi], toep_ref[lo:hi, :], preferred_element_type=F32)
        y = part if y is None else y + part
    xcc = jnp.dot(u, jnp.concatenate([gm, gms], axis=1), preferred_element_type=F32)
    xc_ref[...] = xcc[:, :2 * SSM_STATE]
    xcs_ref[...] = xcc[:, 2 * SSM_STATE:]

    lam = lam_ref[0]
    a, b, c = lam[0:1], lam[1:2], lam[2:3]

    def body(ci, carry):
        st, st_sw = carry
        r = pl.multiple_of(ci * nb, nb)
        x0_ref[pl.ds(r, nb), :] = st
        return (st * a + st_sw * b + xc_ref[pl.ds(r, nb), :],
                st_sw * a + st * c + xcs_ref[pl.ds(r, nb), :])

    z = jnp.zeros((nb, 2 * SSM_STATE), F32)
    lax.fori_loop(0, nc, body, (z, z))

    x0 = x0_ref[...]
    x0_hi = x0.astype(BF16)
    x0_lo = (x0 - x0_hi.astype(F32)).astype(BF16)
    nt = (((1,), (1,)), ((), ()))
    y = y + lax.dot_general(jnp.concatenate([x0_hi, x0_lo], axis=1),
                            jnp.concatenate([hm, hm], axis=1), nt, preferred_element_type=F32)
    for n_out in range(ng):
        y_ref[:, n_out, :] = y[:, n_out * t:(n_out + 1) * t]


def _ssm(ut, tables, *, nb, nc):
    rows, width, t = ut.shape
    ng = SSM_GROUP
    groups = width // ng
    st2 = 2 * SSM_STATE
    gw = ng * t
    grp_spec = pl.BlockSpec((rows, ng, t), lambda g: (0, g, 0))
    tab_specs = [pl.BlockSpec((1,) + tab.shape[1:], lambda g: (g, 0, 0)) for tab in tables]
    return pl.pallas_call(
        functools.partial(_ssm_kernel, nb=nb, nc=nc),
        grid=(groups,),
        in_specs=[grp_spec] + tab_specs,
        out_specs=grp_spec,
        out_shape=jax.ShapeDtypeStruct((rows, width, t), F32),
        scratch_shapes=[
            pltpu.VMEM((gw, gw), BF16),
            pltpu.VMEM((rows, st2), F32),
            pltpu.VMEM((rows, st2), F32),
            pltpu.VMEM((rows, st2), F32),
        ],
        compiler_params=pltpu.CompilerParams(
            dimension_semantics=("parallel",), vmem_limit_bytes=V7X_VMEM_LIMIT_BYTES),
        name="s5_ssm",
    )(ut, *tables)


def _ssm_tables(a_re, a_im, log_dt, b_re, b_im, c_re, c_im, d, t):
    dt = jnp.exp(log_dt)[:, None]
    lre = a_re * dt
    ang = a_im * dt
    mag = jnp.exp(lre)
    lb_re, lb_im = mag * jnp.cos(ang), mag * jnp.sin(ang)
    den = a_re * a_re + a_im * a_im
    nr, ni = lb_re - 1.0, lb_im
    f_re = (nr * a_re + ni * a_im) / den
    f_im = (ni * a_re - nr * a_im) / den
    bb_re = f_re[..., None] * b_re - f_im[..., None] * b_im
    bb_im = f_re[..., None] * b_im + f_im[..., None] * b_re
    bt_re, bt_im = jnp.swapaxes(bb_re, 1, 2), jnp.swapaxes(bb_im, 1, 2)
    cat = lambda x, y: jnp.concatenate([x, y], axis=-1)
    k = jnp.arange(t + 1, dtype=F32)[None, :, None]
    pmag = jnp.exp(k * lre[:, None, :])
    p_re, p_im = pmag * jnp.cos(k * ang[:, None, :]), pmag * jnp.sin(k * ang[:, None, :])
    pa, pb = cat(p_re, p_re), cat(p_im, p_im)
    tr, ti = p_re[:, t], p_im[:, t]
    lam_t = jnp.stack([cat(tr, tr), cat(-ti, ti), cat(ti, -ti)], axis=1)
    lam_t = jnp.concatenate([lam_t, jnp.zeros((lam_t.shape[0], 5, lam_t.shape[2]), F32)], axis=1)
    g = a_re.shape[0]
    d_tab = jnp.broadcast_to(d.reshape(g, SSM_GROUP, 1), (g, SSM_GROUP, t))
    return (
        cat(c_re, -c_im), cat(-c_im, -c_re),
        cat(bt_re, bt_im), cat(-bt_im, bt_re),
        cat(bt_im, bt_re), cat(bt_re, -bt_im),
        cat(bt_re, bt_im),
        d_tab,
        pa[:, :t], pb[:, :t],
        pa[:, 1:], pb[:, 1:],
        pa[:, t - 1::-1][:, :t], pb[:, t - 1::-1][:, :t],
        lam_t,
    )


def _merge_mlp_kernel(x_ref, a_ref, yt_ref, gate_ref, wglu_ref, bglu_ref, wpa_ref, wps_ref,
                      wout_ref, gm_ref, wmi_ref, wmo_ref, o_ref, *, ff_tile):
    d = x_ref.shape[2]
    y = jnp.concatenate([yt_ref[j].T for j in range(yt_ref.shape[0])], axis=0)
    z = jax.nn.gelu(y, approximate=True)
    glu = jnp.dot(z.astype(BF16), wglu_ref[...], preferred_element_type=F32) + bglu_ref[...]
    s = z * jax.nn.sigmoid(glu)
    pa = jnp.dot(a_ref[0], wpa_ref[...], preferred_element_type=F32)
    ps = jnp.dot(s.astype(BF16), wps_ref[...], preferred_element_type=F32)
    merged = gate_ref[0, :, :d].astype(F32) * pa + gate_ref[0, :, d:].astype(F32) * ps
    x1 = x_ref[0] + jnp.dot(merged.astype(BF16), wout_ref[...], preferred_element_type=F32)
    ms = jnp.mean(x1 * x1, axis=-1, keepdims=True)
    hm = (x1 * lax.rsqrt(ms + EPS) * gm_ref[...]).astype(BF16)
    acc = x1
    for j in range(wmi_ref.shape[1] // ff_tile):
        hj = jnp.dot(hm, wmi_ref[:, j * ff_tile:(j + 1) * ff_tile], preferred_element_type=F32)
        hj = jnp.square(jnp.maximum(hj, 0.0)).astype(BF16)
        acc = acc + jnp.dot(hj, wmo_ref[j * ff_tile:(j + 1) * ff_tile, :],
                            preferred_element_type=F32)
    o_ref[0] = acc


def _merge_mlp(x, a, yt, gates, wglu, bglu, wpa, wps, wout, gmlp, wmi, wmo, *, tm, ff_tile):
    bsz, seq, d = x.shape
    tok = lambda b, i: (b, i, 0)
    consts = (wglu, bglu, wpa, wps, wout, gmlp, wmi, wmo)
    return pl.pallas_call(
        functools.partial(_merge_mlp_kernel, ff_tile=ff_tile),
        grid=(bsz, seq // tm),
        in_specs=[
            pl.BlockSpec((1, tm, d), tok), pl.BlockSpec((1, tm, a.shape[2]), tok),
            pl.BlockSpec((tm // SSM_T, None, yt.shape[2], SSM_T), lambda b, i: (i, b, 0, 0)),
            pl.BlockSpec((1, tm, gates.shape[2]), tok),
        ] + [_const_spec(c.shape) for c in consts],
        out_specs=pl.BlockSpec((1, tm, d), tok),
        out_shape=jax.ShapeDtypeStruct((bsz, seq, d), F32),
        compiler_params=pltpu.CompilerParams(
            dimension_semantics=("parallel", "parallel"), vmem_limit_bytes=V7X_VMEM_LIMIT_BYTES),
        name="merge_mlp",
    )(x, a, yt, gates, *consts)


def _lambda_init(layer_idx):
    return 0.8 - 0.6 * math.exp(-0.3 * layer_idx)


def _layer(x, layer_idx, norm_mix_g, w_in, b_gate, q_norm_g, k_norm_g, lambda_q1, lambda_k1,
           lambda_q2, lambda_k2, subln_g, ssm_a_re, ssm_a_im, ssm_log_dt, ssm_b_re, ssm_b_im,
           ssm_c_re, ssm_c_im, ssm_d, w_glu, b_glu, w_proj_attn, w_proj_ssm, w_out,
           norm_mlp_g, w_mlp_in, w_mlp_out):
    bsz, seq, d = x.shape
    qk_w = ATTN_HEADS * 2 * ATTN_QK_DIM
    v_w = ATTN_HEADS * ATTN_V_DIM
    u_w = ssm_d.shape[0]
    row = lambda v: v.reshape(1, -1).astype(F32)
    tm = min(TOKEN_TILE, seq)
    t = min(ATTN_TILE, seq)
    assert tm == t and seq % tm == 0 and tm % SSM_T == 0

    group_mean = jnp.kron(jnp.eye(qk_w // ATTN_QK_DIM, dtype=F32),
                          jnp.full((ATTN_QK_DIM, ATTN_QK_DIM), 1.0 / ATTN_QK_DIM, F32)).astype(BF16)
    reps = qk_w // ATTN_QK_DIM
    q_gain = row(jnp.tile(q_norm_g.astype(F32) * (ATTN_QK_DIM ** -0.5 * math.log2(math.e)), reps))
    k_gain = row(jnp.tile(k_norm_g.astype(F32), reps))
    q, k, vt, ut, gates = _in_proj(
        x, row(norm_mix_g), w_in.astype(BF16), group_mean, q_gain, k_gain, row(b_gate),
        qk_w=qk_w, v_w=v_w, u_w=u_w, tm=tm)

    lam_init = _lambda_init(layer_idx)
    lam = (jnp.exp(jnp.sum(lambda_q1.astype(F32) * lambda_k1.astype(F32)))
           - jnp.exp(jnp.sum(lambda_q2.astype(F32) * lambda_k2.astype(F32))) + lam_init)
    sub_gain = (subln_g.astype(F32) * (1.0 - lam_init)).reshape(-1, 1)
    score_bound = 1.02 * ATTN_QK_DIM * jnp.max(jnp.abs(q_gain)) * jnp.max(jnp.abs(k_gain))
    ctl = jnp.stack([lam, (score_bound <= ATTN_SCORE_LIMIT).astype(F32)])
    a = _attention(ctl, q, k, vt, sub_gain, t=t)

    nc = seq // SSM_T
    tables = _ssm_tables(
        ssm_a_re.astype(F32), ssm_a_im.astype(F32), ssm_log_dt.astype(F32),
        ssm_b_re.astype(F32), ssm_b_im.astype(F32), ssm_c_re.astype(F32), ssm_c_im.astype(F32),
        ssm_d.astype(F32), SSM_T)
    yt = _ssm(ut.reshape(nc * bsz, u_w, SSM_T), tables, nb=bsz, nc=nc)
    yt = yt.reshape(nc, bsz, u_w, SSM_T)

    return _merge_mlp(
        x, a, yt, gates, w_glu.astype(BF16), row(b_glu),
        w_proj_attn.astype(BF16), w_proj_ssm.astype(BF16), w_out.astype(BF16),
        row(norm_mlp_g), w_mlp_in.astype(BF16), w_mlp_out.astype(BF16),
        tm=tm, ff_tile=MLP_FF_TILE)


def kernel(x, norm_mix_g, w_in, b_gate, q_norm_g, k_norm_g, lambda_q1, lambda_k1, lambda_q2, lambda_k2, subln_g, ssm_a_re, ssm_a_im, ssm_log_dt, ssm_b_re, ssm_b_im, ssm_c_re, ssm_c_im, ssm_d, w_glu, b_glu, w_proj_attn, w_proj_ssm, w_out, norm_mlp_g, w_mlp_in, w_mlp_out):
    for l in range(norm_mix_g.shape[0]):
        x = _layer(
            x, l, norm_mix_g[l], w_in[l], b_gate[l], q_norm_g[l], k_norm_g[l],
            lambda_q1[l], lambda_k1[l], lambda_q2[l], lambda_k2[l], subln_g[l],
            ssm_a_re[l], ssm_a_im[l], ssm_log_dt[l], ssm_b_re[l], ssm_b_im[l],
            ssm_c_re[l], ssm_c_im[l], ssm_d[l], w_glu[l], b_glu[l],
            w_proj_attn[l], w_proj_ssm[l], w_out[l],
            norm_mlp_g[l], w_mlp_in[l], w_mlp_out[l])
    return x
```

```python
import functools
import math

import jax
import jax.numpy as jnp
from jax import lax
from jax.experimental import pallas as pl
from jax.experimental.pallas import tpu as pltpu

F32 = jnp.float32
BF16 = jnp.bfloat16

EPS = 1e-6
CHUNK = 64
ATTN_HEADS = 4
ATTN_QK_DIM = 64
ATTN_V_DIM = 2 * ATTN_QK_DIM
SSM_GROUP = 16
SSM_STATE = 64
LANES = 128
SSM_T = LANES
V7X_VMEM_LIMIT_BYTES = 56 * 1024 * 1024

TOKEN_TILE = 512
ATTN_TILE = 512
ATTN_SCORE_LIMIT = 60.0
MLP_FF_TILE = 1024


def _const_spec(shape):
    zeros = (0,) * len(shape)
    return pl.BlockSpec(shape, lambda *_: zeros, pipeline_mode=pl.Buffered(1))


def _in_proj_kernel(x_ref, g_ref, w_ref, bd_ref, qg_ref, kg_ref, bg_ref,
                    q_ref, k_ref, vt_ref, ut_ref, gate_ref, vu_ref, *, qk_w, v_w, u_w):
    x = x_ref[0]
    ms = jnp.mean(x * x, axis=-1, keepdims=True)
    h = (x * lax.rsqrt(ms + EPS) * g_ref[...]).astype(BF16)

    def proj(lo, hi):
        return jnp.dot(h, w_ref[:, lo:hi], preferred_element_type=F32)

    def group_norm(t, gain):
        ms_g = jnp.dot((t * t).astype(BF16), bd_ref[...], preferred_element_type=F32)
        return t * lax.rsqrt(ms_g + EPS) * gain

    o1 = qk_w
    o2 = o1 + qk_w
    o3 = o2 + v_w
    o4 = o3 + u_w
    q_ref[0] = group_norm(proj(0, o1), qg_ref[...]).astype(BF16)
    k_ref[0] = group_norm(proj(o1, o2), kg_ref[...]).astype(BF16)
    vu_ref[...] = proj(o2, o4)
    for hd in range(v_w // ATTN_V_DIM):
        vt_ref[0, hd, 0] = vu_ref[:, hd * ATTN_V_DIM:(hd + 1) * ATTN_V_DIM].T.astype(BF16)
    for j in range(ut_ref.shape[0]):
        ut_ref[j] = vu_ref[j * SSM_T:(j + 1) * SSM_T, v_w:].T
    gate_ref[0] = jax.nn.sigmoid(proj(o4, w_ref.shape[1]) + bg_ref[...]).astype(BF16)


def _in_proj(x, g, w, bd, qg, kg, bg, *, qk_w, v_w, u_w, tm):
    bsz, seq, d = x.shape
    gate_w = w.shape[1] - 2 * qk_w - v_w - u_w
    heads = v_w // ATTN_V_DIM
    tok = lambda b, i: (b, i, 0)
    out_shapes = (
        jax.ShapeDtypeStruct((bsz, seq, qk_w), BF16),
        jax.ShapeDtypeStruct((bsz, seq, qk_w), BF16),
        jax.ShapeDtypeStruct((bsz, heads, seq // tm, ATTN_V_DIM, tm), BF16),
        jax.ShapeDtypeStruct((seq // SSM_T, bsz, u_w, SSM_T), F32),
        jax.ShapeDtypeStruct((bsz, seq, gate_w), BF16),
    )
    return pl.pallas_call(
        functools.partial(_in_proj_kernel, qk_w=qk_w, v_w=v_w, u_w=u_w),
        grid=(bsz, seq // tm),
        in_specs=[
            pl.BlockSpec((1, tm, d), tok),
            _const_spec(g.shape), _const_spec(w.shape), _const_spec(bd.shape),
            _const_spec(qg.shape), _const_spec(kg.shape), _const_spec(bg.shape),
        ],
        out_specs=(
            pl.BlockSpec((1, tm, qk_w), tok), pl.BlockSpec((1, tm, qk_w), tok),
            pl.BlockSpec((1, heads, 1, ATTN_V_DIM, tm), lambda b, i: (b, 0, i, 0, 0)),
            pl.BlockSpec((tm // SSM_T, None, u_w, SSM_T), lambda b, i: (i, b, 0, 0)),
            pl.BlockSpec((1, tm, gate_w), tok),
        ),
        out_shape=out_shapes,
        scratch_shapes=[pltpu.VMEM((tm, v_w + u_w), F32)],
        compiler_params=pltpu.CompilerParams(
            dimension_semantics=("parallel", "parallel"), vmem_limit_bytes=V7X_VMEM_LIMIT_BYTES),
        name="in_proj",
    )(x, g, w, bd, qg, kg, bg)


def _attn_kernel(ctl_ref, q_ref, k_ref, vt_ref, sg_ref, o_ref,
                 qc_ref, m_ref, l_ref, al_ref, acc_ref, l8_ref, s_ref, p_ref, *, t):
    nq = q_ref.shape[1] // t
    bounded = ctl_ref[1] > 0.0
    lane = lax.broadcasted_iota(jnp.int32, (t, ATTN_V_DIM), 1)

    def masked_queries(q):
        zero = jnp.zeros_like(q)
        return jnp.concatenate([jnp.where(lane < ATTN_QK_DIM, q, zero),
                                jnp.where(lane >= ATTN_QK_DIM, q, zero)], axis=0)

    def raw_scores(kt, qc, masked):
        s = lax.dot_general(kt, qc, (((1,), (1,)), ((), ())),
                            preferred_element_type=F32)
        if masked:
            key_chunk = lax.broadcasted_iota(jnp.int32, s.shape, 0) // CHUNK
            qry = lax.broadcasted_iota(jnp.int32, s.shape, 1)
            qry_chunk = jnp.where(qry >= t, qry - t, qry) // CHUNK
            s = jnp.where(key_chunk <= qry_chunk, s, -jnp.inf)
        return s

    def write_output(row0, acc, l):
        o = acc * (1.0 / l)
        o = o[:, :t] - ctl_ref[0] * o[:, t:]
        ms = jnp.mean(o * o, axis=0, keepdims=True)
        o = o * lax.rsqrt(ms + EPS) * sg_ref[...]
        o_ref[0, pl.ds(row0, t), :] = o.T.astype(BF16)

    @pl.when(bounded)
    def _():
        n = 0
        for i in range(nq):
            qc = masked_queries(q_ref[0, i * t:(i + 1) * t, :])
            a = i % 2
            for j in range(i + 1):
                p = jnp.exp2(raw_scores(k_ref[0, j * t:(j + 1) * t, :], qc, j == i))
                part = jnp.sum(p.reshape(t // 8, 8, 2 * t), axis=0)
                p_ref[n % 2] = p.astype(BF16)
                upd = jnp.dot(vt_ref[0, 0, j], p_ref[n % 2], preferred_element_type=F32)
                if j == 0:
                    acc_ref[a] = upd
                    l8_ref[a] = part
                else:
                    acc_ref[a] += upd
                    l8_ref[a] += part
                n += 1
            write_output(i * t, acc_ref[a], jnp.sum(l8_ref[a], axis=0, keepdims=True))

    def general_tile(i, carry):
        row0 = pl.multiple_of(i * t, t)
        qc_ref[...] = masked_queries(q_ref[0, pl.ds(row0, t), :])
        m_ref[...] = jnp.full(m_ref.shape, -jnp.inf, F32)
        l_ref[...] = jnp.zeros(l_ref.shape, F32)
        al_ref[...] = jnp.ones(al_ref.shape, F32)
        acc_ref[0] = jnp.zeros(acc_ref.shape[1:], F32)
        p_ref[1] = jnp.zeros(p_ref.shape[1:], BF16)

        def scores(j, masked, slot):
            kt = k_ref[0, pl.ds(pl.multiple_of(j * t, t), t), :]
            s_ref[slot] = raw_scores(kt, qc_ref[...], masked)

        def softmax(slot):
            s = s_ref[slot]
            m_old = m_ref[...]
            m_new = jnp.maximum(m_old, jnp.max(s, axis=0, keepdims=True))
            alpha = jnp.exp2(m_old - m_new)
            p = jnp.exp2(s - m_new)
            l_ref[...] = alpha * l_ref[...] + jnp.sum(p, axis=0, keepdims=True)
            m_ref[...] = m_new
            al_ref[...] = alpha
            p_ref[slot] = p.astype(BF16)

        def values(j, slot):
            acc_ref[0] = al_ref[...] * acc_ref[0] + jnp.dot(
                vt_ref[0, 0, jnp.maximum(j, 0)], p_ref[slot],
                preferred_element_type=F32)

        def step(j, slot, masked_next):
            values(j - 1, 1 - slot)
            scores(j + 1, masked_next, 1 - slot)
            softmax(slot)

        def finish(j, slot):
            values(j - 1, 1 - slot)
            softmax(slot)
            values(j, slot)

        @pl.when(i > 0)
        def _():
            scores(0, False, 0)

            def body(h, c):
                step(2 * h, 0, False)
                step(2 * h + 1, 1, False)
                return c

            lax.fori_loop(0, (i - 1) // 2, body, 0)

        @pl.when(i % 2 == 1)
        def _():
            step(i - 1, 0, True)
            finish(i, 1)

        @pl.when(jnp.logical_and(i % 2 == 0, i > 0))
        def _():
            step(i - 2, 0, False)
            step(i - 1, 1, True)
            finish(i, 0)

        @pl.when(i == 0)
        def _():
            scores(0, True, 0)
            finish(0, 0)

        write_output(row0, acc_ref[0], l_ref[...])
        return carry

    @pl.when(jnp.logical_not(bounded))
    def _():
        lax.fori_loop(0, nq, general_tile, 0)


def _attention(ctl, q, k, vt, sg, *, t):
    b, s, w = q.shape
    heads = w // ATTN_V_DIM
    head_cols = lambda bi, hi: (bi, 0, hi)
    return pl.pallas_call(
        functools.partial(_attn_kernel, t=t),
        grid=(b, heads),
        in_specs=[
            pl.BlockSpec(memory_space=pltpu.SMEM),
            pl.BlockSpec((1, s, ATTN_V_DIM), head_cols),
            pl.BlockSpec((1, s, ATTN_V_DIM), head_cols),
            pl.BlockSpec((1, 1, s // t, ATTN_V_DIM, t), lambda bi, hi: (bi, hi, 0, 0, 0)),
            _const_spec(sg.shape),
        ],
        out_specs=pl.BlockSpec((1, s, ATTN_V_DIM), head_cols),
        out_shape=jax.ShapeDtypeStruct((b, s, w), BF16),
        scratch_shapes=[
            pltpu.VMEM((2 * t, ATTN_V_DIM), BF16),
            pltpu.VMEM((1, 2 * t), F32),
            pltpu.VMEM((1, 2 * t), F32),
            pltpu.VMEM((1, 2 * t), F32),
            pltpu.VMEM((2, ATTN_V_DIM, 2 * t), F32),
            pltpu.VMEM((2, 8, 2 * t), F32),
            pltpu.VMEM((2, t, 2 * t), F32),
            pltpu.VMEM((2, t, 2 * t), BF16),
        ],
        compiler_params=pltpu.CompilerParams(
            dimension_semantics=("parallel", "parallel"),
            vmem_limit_bytes=V7X_VMEM_LIMIT_BYTES),
        name="diff_attn",
    )(ctl, q, k, vt, sg)


def _ssm_kernel(*refs, nb, nc):
    ng = SSM_GROUP
    (u_ref, ca_ref, cb_ref, ba_ref, bb_ref, bas_ref, bbs_ref, b2_ref, d_ref,
     pa0_ref, pb0_ref, pa1_ref, pb1_ref, qa_ref, qb_ref, lam_ref,
     y_ref, toep_ref, xc_ref, xcs_ref, x0_ref) = refs
    t = SSM_T
    gw = ng * t

    def rep(ref):
        v = ref[0]
        return jnp.broadcast_to(v[:, None, :], (ng, t, v.shape[-1])).reshape(gw, v.shape[-1])

    def tile(ref):
        v = ref[0]
        return jnp.broadcast_to(v[None], (ng, t, v.shape[-1])).reshape(gw, v.shape[-1])

    rca, rcb = rep(ca_ref), rep(cb_ref)
    cl0 = rca * tile(pa0_ref) + rcb * tile(pb0_ref)
    hm = (rca * tile(pa1_ref) + rcb * tile(pb1_ref)).astype(BF16)
    tqa, tqb = tile(qa_ref), tile(qb_ref)
    gm = (rep(ba_ref) * tqa + rep(bb_ref) * tqb).astype(BF16)
    gms = (rep(bas_ref) * tqa + rep(bbs_ref) * tqb).astype(BF16)

    kr = lax.dot_general(b2_ref[0], cl0, (((1,), (1,)), ((), ())),
                         precision=lax.Precision.HIGHEST, preferred_element_type=F32)
    lane = lax.broadcasted_iota(jnp.int32, kr.shape, 1)
    rowi = lax.broadcasted_iota(jnp.int32, kr.shape, 0)
    dfull = jnp.concatenate([d_ref[0]] * ng, axis=1)
    kr = kr + jnp.where(lane == rowi * t, dfull, 0.0)

    u = jnp.concatenate([u_ref[:, n, :] for n in range(ng)], axis=1).astype(BF16)

    causal = (lax.broadcasted_iota(jnp.int32, (t, t), 1)
              >= lax.broadcasted_iota(jnp.int32, (t, t), 0))
    y = None
    for pair in range(ng // 2):
        for n_in in (2 * pair, 2 * pair + 1):
            for n_out in range(ng):
                seg = jnp.broadcast_to(kr[n_in:n_in + 1, n_out * t:(n_out + 1) * t], (t, t))
                blk = pltpu.roll(seg, 0, 1, stride=1, stride_axis=0)
                toep_ref[n_in * t:(n_in + 1) * t, n_out * t:(n_out + 1) * t] = (
                    jnp.where(causal, blk, 0.0).astype(BF16))
        lo, hi = 2 * pair * t, (2 * pair + 2) * t
        part = jnp.dot(u[:, lo:hi], toep_ref[lo:hi, :], preferred_element_type=F32)
        y = part if y is None else y + part
    xcc = jnp.dot(u, jnp.concatenate([gm, gms], axis=1), preferred_element_type=F32)
    xc_ref[...] = xcc[:, :2 * SSM_STATE]
    xcs_ref[...] = xcc[:, 2 * SSM_STATE:]

    lam = lam_ref[0]
    a, b, c = lam[0:1], lam[1:2], lam[2:3]

    def body(ci, carry):
        st, st_sw = carry
        r = pl.multiple_of(ci * nb, nb)
        x0_ref[pl.ds(r, nb), :] = st
        return (st * a + st_sw * b + xc_ref[pl.ds(r, nb), :],
                st_sw * a + st * c + xcs_ref[pl.ds(r, nb), :])

    z = jnp.zeros((nb, 2 * SSM_STATE), F32)
    lax.fori_loop(0, nc, body, (z, z))

    x0 = x0_ref[...]
    x0_hi = x0.astype(BF16)
    x0_lo = (x0 - x0_hi.astype(F32)).astype(BF16)
    nt = (((1,), (1,)), ((), ()))
    y = y + lax.dot_general(jnp.concatenate([x0_hi, x0_lo], axis=1),
                            jnp.concatenate([hm, hm], axis=1), nt, preferred_element_type=F32)
    for n_out in range(ng):
        y_ref[:, n_out, :] = y[:, n_out * t:(n_out + 1) * t]


def _ssm(ut, tables, *, nb, nc):
    rows, width, t = ut.shape
    ng = SSM_GROUP
    groups = width // ng
    st2 = 2 * SSM_STATE
    gw = ng * t
    grp_spec = pl.BlockSpec((rows, ng, t), lambda g: (0, g, 0))
    tab_specs = [pl.BlockSpec((1,) + tab.shape[1:], lambda g: (g, 0, 0)) for tab in tables]
    return pl.pallas_call(
        functools.partial(_ssm_kernel, nb=nb, nc=nc),
        grid=(groups,),
        in_specs=[grp_spec] + tab_specs,
        out_specs=grp_spec,
        out_shape=jax.ShapeDtypeStruct((rows, width, t), F32),
        scratch_shapes=[
            pltpu.VMEM((gw, gw), BF16),
            pltpu.VMEM((rows, st2), F32),
            pltpu.VMEM((rows, st2), F32),
            pltpu.VMEM((rows, st2), F32),
        ],
        compiler_params=pltpu.CompilerParams(
            dimension_semantics=("parallel",), vmem_limit_bytes=V7X_VMEM_LIMIT_BYTES),
        name="s5_ssm",
    )(ut, *tables)


def _ssm_tables(a_re, a_im, log_dt, b_re, b_im, c_re, c_im, d, t):
    dt = jnp.exp(log_dt)[:, None]
    lre = a_re * dt
    ang = a_im * dt
    mag = jnp.exp(lre)
    lb_re, lb_im = mag * jnp.cos(ang), mag * jnp.sin(ang)
    den = a_re * a_re + a_im * a_im
    nr, ni = lb_re - 1.0, lb_im
    f_re = (nr * a_re + ni * a_im) / den
    f_im = (ni * a_re - nr * a_im) / den
    bb_re = f_re[..., None] * b_re - f_im[..., None] * b_im
    bb_im = f_re[..., None] * b_im + f_im[..., None] * b_re
    bt_re, bt_im = jnp.swapaxes(bb_re, 1, 2), jnp.swapaxes(bb_im, 1, 2)
    cat = lambda x, y: jnp.concatenate([x, y], axis=-1)
    k = jnp.arange(t + 1, dtype=F32)[None, :, None]
    pmag = jnp.exp(k * lre[:, None, :])
    p_re, p_im = pmag * jnp.cos(k * ang[:, None, :]), pmag * jnp.sin(k * ang[:, None, :])
    pa, pb = cat(p_re, p_re), cat(p_im, p_im)
    tr, ti = p_re[:, t], p_im[:, t]
    lam_t = jnp.stack([cat(tr, tr), cat(-ti, ti), cat(ti, -ti)], axis=1)
    lam_t = jnp.concatenate([lam_t, jnp.zeros((lam_t.shape[0], 5, lam_t.shape[2]), F32)], axis=1)
    g = a_re.shape[0]
    d_tab = jnp.broadcast_to(d.reshape(g, SSM_GROUP, 1), (g, SSM_GROUP, t))
    return (
        cat(c_re, -c_im), cat(-c_im, -c_re),
        cat(bt_re, bt_im), cat(-bt_im, bt_re),
        cat(bt_im, bt_re), cat(bt_re, -bt_im),
        cat(bt_re, bt_im),
        d_tab,
        pa[:, :t], pb[:, :t],
        pa[:, 1:], pb[:, 1:],
        pa[:, t - 1::-1][:, :t], pb[:, t - 1::-1][:, :t],
        lam_t,
    )


def _merge_mlp_kernel(x_ref, a_ref, yt_ref, gate_ref, wglu_ref, bglu_ref, wpa_ref, wps_ref,
                      wout_ref, gm_ref, wmi_ref, wmo_ref, o_ref, *, ff_tile):
    d = x_ref.shape[2]
    y = jnp.concatenate([yt_ref[j].T for j in range(yt_ref.shape[0])], axis=0)
    z = jax.nn.gelu(y, approximate=True)
    glu = jnp.dot(z.astype(BF16), wglu_ref[...], preferred_element_type=F32) + bglu_ref[...]
    s = z * jax.nn.sigmoid(glu)
    pa = jnp.dot(a_ref[0], wpa_ref[...], preferred_element_type=F32)
    ps = jnp.dot(s.astype(BF16), wps_ref[...], preferred_element_type=F32)
    merged = gate_ref[0, :, :d].astype(F32) * pa + gate_ref[0, :, d:].astype(F32) * ps
    x1 = x_ref[0] + jnp.dot(merged.astype(BF16), wout_ref[...], preferred_element_type=F32)
    ms = jnp.mean(x1 * x1, axis=-1, keepdims=True)
    hm = (x1 * lax.rsqrt(ms + EPS) * gm_ref[...]).astype(BF16)
    acc = x1
    for j in range(wmi_ref.shape[1] // ff_tile):
        hj = jnp.dot(hm, wmi_ref[:, j * ff_tile:(j + 1) * ff_tile], preferred_element_type=F32)
        hj = jnp.square(jnp.maximum(hj, 0.0)).astype(BF16)
        acc = acc + jnp.dot(hj, wmo_ref[j * ff_tile:(j + 1) * ff_tile, :],
                            preferred_element_type=F32)
    o_ref[0] = acc


def _merge_mlp(x, a, yt, gates, wglu, bglu, wpa, wps, wout, gmlp, wmi, wmo, *, tm, ff_tile):
    bsz, seq, d = x.shape
    tok = lambda b, i: (b, i, 0)
    consts = (wglu, bglu, wpa, wps, wout, gmlp, wmi, wmo)
    return pl.pallas_call(
        functools.partial(_merge_mlp_kernel, ff_tile=ff_tile),
        grid=(bsz, seq // tm),
        in_specs=[
            pl.BlockSpec((1, tm, d), tok), pl.BlockSpec((1, tm, a.shape[2]), tok),
            pl.BlockSpec((tm // SSM_T, None, yt.shape[2], SSM_T), lambda b, i: (i, b, 0, 0)),
            pl.BlockSpec((1, tm, gates.shape[2]), tok),
        ] + [_const_spec(c.shape) for c in consts],
        out_specs=pl.BlockSpec((1, tm, d), tok),
        out_shape=jax.ShapeDtypeStruct((bsz, seq, d), F32),
        compiler_params=pltpu.CompilerParams(
            dimension_semantics=("parallel", "parallel"), vmem_limit_bytes=V7X_VMEM_LIMIT_BYTES),
        name="merge_mlp",
    )(x, a, yt, gates, *consts)


def _lambda_init(layer_idx):
    return 0.8 - 0.6 * math.exp(-0.3 * layer_idx)


def _layer(x, layer_idx, norm_mix_g, w_in, b_gate, q_norm_g, k_norm_g, lambda_q1, lambda_k1,
           lambda_q2, lambda_k2, subln_g, ssm_a_re, ssm_a_im, ssm_log_dt, ssm_b_re, ssm_b_im,
           ssm_c_re, ssm_c_im, ssm_d, w_glu, b_glu, w_proj_attn, w_proj_ssm, w_out,
           norm_mlp_g, w_mlp_in, w_mlp_out):
    bsz, seq, d = x.shape
    qk_w = ATTN_HEADS * 2 * ATTN_QK_DIM
    v_w = ATTN_HEADS * ATTN_V_DIM
    u_w = ssm_d.shape[0]
    row = lambda v: v.reshape(1, -1).astype(F32)
    tm = min(TOKEN_TILE, seq)
    t = min(ATTN_TILE, seq)
    assert tm == t and seq % tm == 0 and tm % SSM_T == 0

    group_mean = jnp.kron(jnp.eye(qk_w // ATTN_QK_DIM, dtype=F32),
                          jnp.full((ATTN_QK_DIM, ATTN_QK_DIM), 1.0 / ATTN_QK_DIM, F32)).astype(BF16)
    reps = qk_w // ATTN_QK_DIM
    q_gain = row(jnp.tile(q_norm_g.astype(F32) * (ATTN_QK_DIM ** -0.5 * math.log2(math.e)), reps))
    k_gain = row(jnp.tile(k_norm_g.astype(F32), reps))
    q, k, vt, ut, gates = _in_proj(
        x, row(norm_mix_g), w_in.astype(BF16), group_mean, q_gain, k_gain, row(b_gate),
        qk_w=qk_w, v_w=v_w, u_w=u_w, tm=tm)

    lam_init = _lambda_init(layer_idx)
    lam = (jnp.exp(jnp.sum(lambda_q1.astype(F32) * lambda_k1.astype(F32)))
           - jnp.exp(jnp.sum(lambda_q2.astype(F32) * lambda_k2.astype(F32))) + lam_init)
    sub_gain = (subln_g.astype(F32) * (1.0 - lam_init)).reshape(-1, 1)
    score_bound = 1.02 * ATTN_QK_DIM * jnp.max(jnp.abs(q_gain)) * jnp.max(jnp.abs(k_gain))
    ctl = jnp.stack([lam, (score_bound <= ATTN_SCORE_LIMIT).astype(F32)])
    a = _attention(ctl, q, k, vt, sub_gain, t=t)

    nc = seq // SSM_T
    tables = _ssm_tables(
        ssm_a_re.astype(F32), ssm_a_im.astype(F32), ssm_log_dt.astype(F32),
        ssm_b_re.astype(F32), ssm_b_im.astype(F32), ssm_c_re.astype(F32), ssm_c_im.astype(F32),
        ssm_d.astype(F32), SSM_T)
    yt = _ssm(ut.reshape(nc * bsz, u_w, SSM_T), tables, nb=bsz, nc=nc)
    yt = yt.reshape(nc, bsz, u_w, SSM_T)

    return _merge_mlp(
        x, a, yt, gates, w_glu.astype(BF16), row(b_glu),
        w_proj_attn.astype(BF16), w_proj_ssm.astype(BF16), w_out.astype(BF16),
        row(norm_mlp_g), w_mlp_in.astype(BF16), w_mlp_out.astype(BF16),
        tm=tm, ff_tile=MLP_FF_TILE)


def kernel(x, norm_mix_g, w_in, b_gate, q_norm_g, k_norm_g, lambda_q1, lambda_k1, lambda_q2, lambda_k2, subln_g, ssm_a_re, ssm_a_im, ssm_log_dt, ssm_b_re, ssm_b_im, ssm_c_re, ssm_c_im, ssm_d, w_glu, b_glu, w_proj_attn, w_proj_ssm, w_out, norm_mlp_g, w_mlp_in, w_mlp_out):
    for l in range(norm_mix_g.shape[0]):
        x = _layer(
            x, l, norm_mix_g[l], w_in[l], b_gate[l], q_norm_g[l], k_norm_g[l],
            lambda_q1[l], lambda_k1[l], lambda_q2[l], lambda_k2[l], subln_g[l],
            ssm_a_re[l], ssm_a_im[l], ssm_log_dt[l], ssm_b_re[l], ssm_b_im[l],
            ssm_c_re[l], ssm_c_im[l], ssm_d[l], w_glu[l], b_glu[l],
            w_proj_attn[l], w_proj_ssm[l], w_out[l],
            norm_mlp_g[l], w_mlp_in[l], w_mlp_out[l])
    return x
```

```python
import functools
import math

import jax
import jax.numpy as jnp
from jax import lax
from jax.experimental import pallas as pl
from jax.experimental.pallas import tpu as pltpu

F32 = jnp.float32
BF16 = jnp.bfloat16

EPS = 1e-6
CHUNK = 64
ATTN_HEADS = 4
ATTN_QK_DIM = 64
ATTN_V_DIM = 2 * ATTN_QK_DIM
SSM_GROUP = 16
SSM_STATE = 64
LANES = 128
MXU_TILE = 256
SSM_T = LANES
V7X_VMEM_LIMIT_BYTES = 56 * 1024 * 1024

TOKEN_TILE = 512
ATTN_TILE = 512
ATTN_SCORE_LIMIT = 60.0
MLP_FF_TILE = 1024


def _const_spec(shape):
    zeros = (0,) * len(shape)
    return pl.BlockSpec(shape, lambda *_: zeros, pipeline_mode=pl.Buffered(1))


def _in_proj_kernel(x_ref, g_ref, w_ref, bd_ref, qg_ref, kg_ref, bg_ref,
                    q_ref, k_ref, vt_ref, ut_ref, gate_ref, vu_ref, *, qk_w, v_w, u_w):
    x = x_ref[0]
    ms = jnp.mean(x * x, axis=-1, keepdims=True)
    h = (x * lax.rsqrt(ms + EPS) * g_ref[...]).astype(BF16)

    def proj(lo, hi):
        return jnp.dot(h, w_ref[:, lo:hi], preferred_element_type=F32)

    def group_norm(t, gain):
        sq = (t * t).astype(BF16)
        w = bd_ref.shape[0]
        ms_g = jnp.concatenate(
            [jnp.dot(sq[:, c * w:(c + 1) * w], bd_ref[...], preferred_element_type=F32)
             for c in range(sq.shape[1] // w)], axis=1)
        return t * lax.rsqrt(ms_g + EPS) * gain

    o1 = qk_w
    o2 = o1 + qk_w
    o3 = o2 + v_w
    o4 = o3 + u_w
    q_ref[0] = group_norm(proj(0, o1), qg_ref[...]).astype(BF16)
    k_ref[0] = group_norm(proj(o1, o2), kg_ref[...]).astype(BF16)
    vu_ref[...] = proj(o2, o4)
    for hd in range(v_w // ATTN_V_DIM):
        vt_ref[0, hd, 0] = vu_ref[:, hd * ATTN_V_DIM:(hd + 1) * ATTN_V_DIM].T.astype(BF16)
    for j in range(ut_ref.shape[0]):
        ut_ref[j] = vu_ref[j * SSM_T:(j + 1) * SSM_T, v_w:].T
    gate_ref[0] = jax.nn.sigmoid(proj(o4, w_ref.shape[1]) + bg_ref[...]).astype(BF16)


def _in_proj(x, g, w, bd, qg, kg, bg, *, qk_w, v_w, u_w, tm):
    bsz, seq, d = x.shape
    gate_w = w.shape[1] - 2 * qk_w - v_w - u_w
    heads = v_w // ATTN_V_DIM
    tok = lambda b, i: (b, i, 0)
    out_shapes = (
        jax.ShapeDtypeStruct((bsz, seq, qk_w), BF16),
        jax.ShapeDtypeStruct((bsz, seq, qk_w), BF16),
        jax.ShapeDtypeStruct((bsz, heads, seq // tm, ATTN_V_DIM, tm), BF16),
        jax.ShapeDtypeStruct((seq // SSM_T, bsz, u_w, SSM_T), F32),
        jax.ShapeDtypeStruct((bsz, seq, gate_w), BF16),
    )
    return pl.pallas_call(
        functools.partial(_in_proj_kernel, qk_w=qk_w, v_w=v_w, u_w=u_w),
        grid=(bsz, seq // tm),
        in_specs=[
            pl.BlockSpec((1, tm, d), tok),
            _const_spec(g.shape), _const_spec(w.shape), _const_spec(bd.shape),
            _const_spec(qg.shape), _const_spec(kg.shape), _const_spec(bg.shape),
        ],
        out_specs=(
            pl.BlockSpec((1, tm, qk_w), tok), pl.BlockSpec((1, tm, qk_w), tok),
            pl.BlockSpec((1, heads, 1, ATTN_V_DIM, tm), lambda b, i: (b, 0, i, 0, 0)),
            pl.BlockSpec((tm // SSM_T, None, u_w, SSM_T), lambda b, i: (i, b, 0, 0)),
            pl.BlockSpec((1, tm, gate_w), tok),
        ),
        out_shape=out_shapes,
        scratch_shapes=[pltpu.VMEM((tm, v_w + u_w), F32)],
        compiler_params=pltpu.CompilerParams(
            dimension_semantics=("parallel", "parallel"), vmem_limit_bytes=V7X_VMEM_LIMIT_BYTES),
        name="in_proj",
    )(x, g, w, bd, qg, kg, bg)


def _attn_kernel(ctl_ref, q_ref, k_ref, vt_ref, sg_ref, o_ref,
                 qc_ref, m_ref, l_ref, al_ref, acc_ref, l8_ref, s_ref, p_ref, *, t):
    nq = q_ref.shape[1] // t
    bounded = ctl_ref[1] > 0.0
    lane = lax.broadcasted_iota(jnp.int32, (t, ATTN_V_DIM), 1)

    def masked_queries(q):
        zero = jnp.zeros_like(q)
        return jnp.concatenate([jnp.where(lane < ATTN_QK_DIM, q, zero),
                                jnp.where(lane >= ATTN_QK_DIM, q, zero)], axis=0)

    def raw_scores(kt, qc, masked):
        s = lax.dot_general(kt, qc, (((1,), (1,)), ((), ())),
                            preferred_element_type=F32)
        if masked:
            per_map = qc.shape[0] // 2
            key_chunk = lax.broadcasted_iota(jnp.int32, s.shape, 0) // CHUNK
            qry = lax.broadcasted_iota(jnp.int32, s.shape, 1)
            qry_chunk = jnp.where(qry >= per_map, qry - per_map, qry) // CHUNK
            s = jnp.where(key_chunk <= qry_chunk, s, -jnp.inf)
        return s

    def write_output(row0, acc, l):
        o = acc * (1.0 / l)
        o = o[:, :t] - ctl_ref[0] * o[:, t:]
        ms = jnp.mean(o * o, axis=0, keepdims=True)
        o = o * lax.rsqrt(ms + EPS) * sg_ref[...]
        o_ref[0, pl.ds(row0, t), :] = o.T.astype(BF16)

    @pl.when(bounded)
    def _():
        n = 0
        for i in range(nq):
            qc = masked_queries(q_ref[0, i * t:(i + 1) * t, :])
            a = i % 2
            for j in range(i + 1):
                p = jnp.exp2(raw_scores(k_ref[0, j * t:(j + 1) * t, :], qc, j == i))
                part = jnp.sum(p.reshape(t // 8, 8, 2 * t), axis=0)
                p_ref[n % 2] = p.astype(BF16)
                upd = jnp.dot(vt_ref[0, 0, j], p_ref[n % 2], preferred_element_type=F32)
                if j == 0:
                    acc_ref[a] = upd
                    l8_ref[a] = part
                else:
                    acc_ref[a] += upd
                    l8_ref[a] += part
                n += 1
            write_output(i * t, acc_ref[a], jnp.sum(l8_ref[a], axis=0, keepdims=True))

    def general_tile(i, carry):
        row0 = pl.multiple_of(i * t, t)
        qc_ref[...] = masked_queries(q_ref[0, pl.ds(row0, t), :])
        m_ref[...] = jnp.full(m_ref.shape, -jnp.inf, F32)
        l_ref[...] = jnp.zeros(l_ref.shape, F32)
        al_ref[...] = jnp.ones(al_ref.shape, F32)
        acc_ref[0] = jnp.zeros(acc_ref.shape[1:], F32)
        p_ref[1] = jnp.zeros(p_ref.shape[1:], BF16)

        def scores(j, masked, slot):
            kt = k_ref[0, pl.ds(pl.multiple_of(j * t, t), t), :]
            s_ref[slot] = raw_scores(kt, qc_ref[...], masked)

        def softmax(slot):
            s = s_ref[slot]
            m_old = m_ref[...]
            m_new = jnp.maximum(m_old, jnp.max(s, axis=0, keepdims=True))
            alpha = jnp.exp2(m_old - m_new)
            p = jnp.exp2(s - m_new)
            l_ref[...] = alpha * l_ref[...] + jnp.sum(p, axis=0, keepdims=True)
            m_ref[...] = m_new
            al_ref[...] = alpha
            p_ref[slot] = p.astype(BF16)

        def values(j, slot):
            acc_ref[0] = al_ref[...] * acc_ref[0] + jnp.dot(
                vt_ref[0, 0, jnp.maximum(j, 0)], p_ref[slot],
                preferred_element_type=F32)

        def step(j, slot, masked_next):
            values(j - 1, 1 - slot)
            scores(j + 1, masked_next, 1 - slot)
            softmax(slot)

        def finish(j, slot):
            values(j - 1, 1 - slot)
            softmax(slot)
            values(j, slot)

        @pl.when(i > 0)
        def _():
            scores(0, False, 0)

            def body(h, c):
                step(2 * h, 0, False)
                step(2 * h + 1, 1, False)
                return c

            lax.fori_loop(0, (i - 1) // 2, body, 0)

        @pl.when(i % 2 == 1)
        def _():
            step(i - 1, 0, True)
            finish(i, 1)

        @pl.when(jnp.logical_and(i % 2 == 0, i > 0))
        def _():
            step(i - 2, 0, False)
            step(i - 1, 1, True)
            finish(i, 0)

        @pl.when(i == 0)
        def _():
            scores(0, True, 0)
            finish(0, 0)

        write_output(row0, acc_ref[0], l_ref[...])
        return carry

    @pl.when(jnp.logical_not(bounded))
    def _():
        lax.fori_loop(0, nq, general_tile, 0)


def _attention(ctl, q, k, vt, sg, *, t):
    b, s, w = q.shape
    heads = w // ATTN_V_DIM
    head_cols = lambda bi, hi: (bi, 0, hi)
    return pl.pallas_call(
        functools.partial(_attn_kernel, t=t),
        grid=(b, heads),
        in_specs=[
            pl.BlockSpec(memory_space=pltpu.SMEM),
            pl.BlockSpec((1, s, ATTN_V_DIM), head_cols),
            pl.BlockSpec((1, s, ATTN_V_DIM), head_cols),
            pl.BlockSpec((1, 1, s // t, ATTN_V_DIM, t), lambda bi, hi: (bi, hi, 0, 0, 0)),
            _const_spec(sg.shape),
        ],
        out_specs=pl.BlockSpec((1, s, ATTN_V_DIM), head_cols),
        out_shape=jax.ShapeDtypeStruct((b, s, w), BF16),
        scratch_shapes=[
            pltpu.VMEM((2 * t, ATTN_V_DIM), BF16),
            pltpu.VMEM((1, 2 * t), F32),
            pltpu.VMEM((1, 2 * t), F32),
            pltpu.VMEM((1, 2 * t), F32),
            pltpu.VMEM((2, ATTN_V_DIM, 2 * t), F32),
            pltpu.VMEM((2, 8, 2 * t), F32),
            pltpu.VMEM((2, t, 2 * t), F32),
            pltpu.VMEM((2, t, 2 * t), BF16),
        ],
        compiler_params=pltpu.CompilerParams(
            dimension_semantics=("parallel", "parallel"),
            vmem_limit_bytes=V7X_VMEM_LIMIT_BYTES),
        name="diff_attn",
    )(ctl, q, k, vt, sg)


def _ssm_kernel(*refs, nb, nc):
    ng = SSM_GROUP
    (u_ref, ca_ref, cb_ref, ba_ref, bb_ref, bas_ref, bbs_ref, b2_ref, d_ref,
     pa0_ref, pb0_ref, pa1_ref, pb1_ref, qa_ref, qb_ref, lam_ref,
     y_ref, toep_ref, xc_ref, xcs_ref, x0_ref) = refs
    t = SSM_T
    gw = ng * t

    def rep(ref):
        v = ref[0]
        return jnp.broadcast_to(v[:, None, :], (ng, t, v.shape[-1])).reshape(gw, v.shape[-1])

    def tile(ref):
        v = ref[0]
        return jnp.broadcast_to(v[None], (ng, t, v.shape[-1])).reshape(gw, v.shape[-1])

    rca, rcb = rep(ca_ref), rep(cb_ref)
    cl0 = rca * tile(pa0_ref) + rcb * tile(pb0_ref)
    hm = (rca * tile(pa1_ref) + rcb * tile(pb1_ref)).astype(BF16)
    tqa, tqb = tile(qa_ref), tile(qb_ref)
    gm = (rep(ba_ref) * tqa + rep(bb_ref) * tqb).astype(BF16)
    gms = (rep(bas_ref) * tqa + rep(bbs_ref) * tqb).astype(BF16)

    kr = lax.dot_general(b2_ref[0], cl0, (((1,), (1,)), ((), ())),
                         precision=lax.Precision.HIGHEST, preferred_element_type=F32)
    lane = lax.broadcasted_iota(jnp.int32, kr.shape, 1)
    rowi = lax.broadcasted_iota(jnp.int32, kr.shape, 0)
    dfull = jnp.concatenate([d_ref[0]] * ng, axis=1)
    kr = kr + jnp.where(lane == rowi * t, dfull, 0.0)

    u = jnp.concatenate([u_ref[:, n, :] for n in range(ng)], axis=1).astype(BF16)

    causal = (lax.broadcasted_iota(jnp.int32, (t, t), 1)
              >= lax.broadcasted_iota(jnp.int32, (t, t), 0))
    y = None
    for pair in range(ng // 2):
        for n_in in (2 * pair, 2 * pair + 1):
            for n_out in range(ng):
                seg = jnp.broadcast_to(kr[n_in:n_in + 1, n_out * t:(n_out + 1) * t], (t, t))
                blk = pltpu.roll(seg, 0, 1, stride=1, stride_axis=0)
                toep_ref[n_in * t:(n_in + 1) * t, n_out * t:(n_out + 1) * t] = (
                    jnp.where(causal, blk, 0.0).astype(BF16))
        lo, hi = 2 * pair * t, (2 * pair + 2) * t
        part = jnp.dot(u[:, lo:hi], toep_ref[lo:hi, :], preferred_element_type=F32)
        y = part if y is None else y + part
    xcc = jnp.dot(u, jnp.concatenate([gm, gms], axis=1), preferred_element_type=F32)
    xc_ref[...] = xcc[:, :2 * SSM_STATE]
    xcs_ref[...] = xcc[:, 2 * SSM_STATE:]

    lam = lam_ref[0]
    a, b, c = lam[0:1], lam[1:2], lam[2:3]

    def body(ci, carry):
        st, st_sw = carry
        r = pl.multiple_of(ci * nb, nb)
        x0_ref[pl.ds(r, nb), :] = st
        return (st * a + st_sw * b + xc_ref[pl.ds(r, nb), :],
                st_sw * a + st * c + xcs_ref[pl.ds(r, nb), :])

    z = jnp.zeros((nb, 2 * SSM_STATE), F32)
    lax.fori_loop(0, nc, body, (z, z))

    x0 = x0_ref[...]
    x0_hi = x0.astype(BF16)
    x0_lo = (x0 - x0_hi.astype(F32)).astype(BF16)
    nt = (((1,), (1,)), ((), ()))
    y = y + lax.dot_general(jnp.concatenate([x0_hi, x0_lo], axis=1),
                            jnp.concatenate([hm, hm], axis=1), nt, preferred_element_type=F32)
    for n_out in range(ng):
        y_ref[:, n_out, :] = y[:, n_out * t:(n_out + 1) * t]


def _ssm(ut, tables, *, nb, nc):
    rows, width, t = ut.shape
    ng = SSM_GROUP
    groups = width // ng
    st2 = 2 * SSM_STATE
    gw = ng * t
    grp_spec = pl.BlockSpec((rows, ng, t), lambda g: (0, g, 0))
    tab_specs = [pl.BlockSpec((1,) + tab.shape[1:], lambda g: (g, 0, 0)) for tab in tables]
    return pl.pallas_call(
        functools.partial(_ssm_kernel, nb=nb, nc=nc),
        grid=(groups,),
        in_specs=[grp_spec] + tab_specs,
        out_specs=grp_spec,
        out_shape=jax.ShapeDtypeStruct((rows, width, t), F32),
        scratch_shapes=[
            pltpu.VMEM((gw, gw), BF16),
            pltpu.VMEM((rows, st2), F32),
            pltpu.VMEM((rows, st2), F32),
            pltpu.VMEM((rows, st2), F32),
        ],
        compiler_params=pltpu.CompilerParams(
            dimension_semantics=("parallel",), vmem_limit_bytes=V7X_VMEM_LIMIT_BYTES),
        name="s5_ssm",
    )(ut, *tables)


def _ssm_tables(a_re, a_im, log_dt, b_re, b_im, c_re, c_im, d, t):
    dt = jnp.exp(log_dt)[:, None]
    lre = a_re * dt
    ang = a_im * dt
    mag = jnp.exp(lre)
    lb_re, lb_im = mag * jnp.cos(ang), mag * jnp.sin(ang)
    den = a_re * a_re + a_im * a_im
    nr, ni = lb_re - 1.0, lb_im
    f_re = (nr * a_re + ni * a_im) / den
    f_im = (ni * a_re - nr * a_im) / den
    bb_re = f_re[..., None] * b_re - f_im[..., None] * b_im
    bb_im = f_re[..., None] * b_im + f_im[..., None] * b_re
    bt_re, bt_im = jnp.swapaxes(bb_re, 1, 2), jnp.swapaxes(bb_im, 1, 2)
    cat = lambda x, y: jnp.concatenate([x, y], axis=-1)
    k = jnp.arange(t + 1, dtype=F32)[None, :, None]
    pmag = jnp.exp(k * lre[:, None, :])
    p_re, p_im = pmag * jnp.cos(k * ang[:, None, :]), pmag * jnp.sin(k * ang[:, None, :])
    pa, pb = cat(p_re, p_re), cat(p_im, p_im)
    tr, ti = p_re[:, t], p_im[:, t]
    lam_t = jnp.stack([cat(tr, tr), cat(-ti, ti), cat(ti, -ti)], axis=1)
    lam_t = jnp.concatenate([lam_t, jnp.zeros((lam_t.shape[0], 5, lam_t.shape[2]), F32)], axis=1)
    g = a_re.shape[0]
    d_tab = jnp.broadcast_to(d.reshape(g, SSM_GROUP, 1), (g, SSM_GROUP, t))
    return (
        cat(c_re, -c_im), cat(-c_im, -c_re),
        cat(bt_re, bt_im), cat(-bt_im, bt_re),
        cat(bt_im, bt_re), cat(bt_re, -bt_im),
        cat(bt_re, bt_im),
        d_tab,
        pa[:, :t], pb[:, :t],
        pa[:, 1:], pb[:, 1:],
        pa[:, t - 1::-1][:, :t], pb[:, t - 1::-1][:, :t],
        lam_t,
    )


def _merge_mlp_kernel(x_ref, a_ref, yt_ref, gate_ref, wglu_ref, bglu_ref, wpa_ref, wps_ref,
                      wout_ref, gm_ref, wmi_ref, wmo_ref, o_ref, *, ff_tile):
    d = x_ref.shape[2]
    y = jnp.concatenate([yt_ref[j].T for j in range(yt_ref.shape[0])], axis=0)
    z = jax.nn.gelu(y, approximate=True)
    glu = jnp.dot(z.astype(BF16), wglu_ref[...], preferred_element_type=F32) + bglu_ref[...]
    s = z * jax.nn.sigmoid(glu)
    pa = jnp.dot(a_ref[0], wpa_ref[...], preferred_element_type=F32)
    ps = jnp.dot(s.astype(BF16), wps_ref[...], preferred_element_type=F32)
    merged = gate_ref[0, :, :d].astype(F32) * pa + gate_ref[0, :, d:].astype(F32) * ps
    x1 = x_ref[0] + jnp.dot(merged.astype(BF16), wout_ref[...], preferred_element_type=F32)
    ms = jnp.mean(x1 * x1, axis=-1, keepdims=True)
    hm = (x1 * lax.rsqrt(ms + EPS) * gm_ref[...]).astype(BF16)
    acc = x1
    for j in range(wmi_ref.shape[1] // ff_tile):
        hj = jnp.dot(hm, wmi_ref[:, j * ff_tile:(j + 1) * ff_tile], preferred_element_type=F32)
        hj = jnp.square(jnp.maximum(hj, 0.0)).astype(BF16)
        acc = acc + jnp.dot(hj, wmo_ref[j * ff_tile:(j + 1) * ff_tile, :],
                            preferred_element_type=F32)
    o_ref[0] = acc


def _merge_mlp(x, a, yt, gates, wglu, bglu, wpa, wps, wout, gmlp, wmi, wmo, *, tm, ff_tile):
    bsz, seq, d = x.shape
    tok = lambda b, i: (b, i, 0)
    consts = (wglu, bglu, wpa, wps, wout, gmlp, wmi, wmo)
    return pl.pallas_call(
        functools.partial(_merge_mlp_kernel, ff_tile=ff_tile),
        grid=(bsz, seq // tm),
        in_specs=[
            pl.BlockSpec((1, tm, d), tok), pl.BlockSpec((1, tm, a.shape[2]), tok),
            pl.BlockSpec((tm // SSM_T, None, yt.shape[2], SSM_T), lambda b, i: (i, b, 0, 0)),
            pl.BlockSpec((1, tm, gates.shape[2]), tok),
        ] + [_const_spec(c.shape) for c in consts],
        out_specs=pl.BlockSpec((1, tm, d), tok),
        out_shape=jax.ShapeDtypeStruct((bsz, seq, d), F32),
        compiler_params=pltpu.CompilerParams(
            dimension_semantics=("parallel", "parallel"), vmem_limit_bytes=V7X_VMEM_LIMIT_BYTES),
        name="merge_mlp",
    )(x, a, yt, gates, *consts)


def _lambda_init(layer_idx):
    return 0.8 - 0.6 * math.exp(-0.3 * layer_idx)


def _layer(x, layer_idx, norm_mix_g, w_in, b_gate, q_norm_g, k_norm_g, lambda_q1, lambda_k1,
           lambda_q2, lambda_k2, subln_g, ssm_a_re, ssm_a_im, ssm_log_dt, ssm_b_re, ssm_b_im,
           ssm_c_re, ssm_c_im, ssm_d, w_glu, b_glu, w_proj_attn, w_proj_ssm, w_out,
           norm_mlp_g, w_mlp_in, w_mlp_out):
    bsz, seq, d = x.shape
    qk_w = ATTN_HEADS * 2 * ATTN_QK_DIM
    v_w = ATTN_HEADS * ATTN_V_DIM
    u_w = ssm_d.shape[0]
    row = lambda v: v.reshape(1, -1).astype(F32)
    tm = min(TOKEN_TILE, seq)
    t = min(ATTN_TILE, seq)
    assert tm == t and seq % tm == 0 and tm % SSM_T == 0

    group_mean = jnp.kron(jnp.eye(MXU_TILE // ATTN_QK_DIM, dtype=F32),
                          jnp.full((ATTN_QK_DIM, ATTN_QK_DIM), 1.0 / ATTN_QK_DIM, F32)).astype(BF16)
    reps = qk_w // ATTN_QK_DIM
    q_gain = row(jnp.tile(q_norm_g.astype(F32) * (ATTN_QK_DIM ** -0.5 * math.log2(math.e)), reps))
    k_gain = row(jnp.tile(k_norm_g.astype(F32), reps))
    q, k, vt, ut, gates = _in_proj(
        x, row(norm_mix_g), w_in.astype(BF16), group_mean, q_gain, k_gain, row(b_gate),
        qk_w=qk_w, v_w=v_w, u_w=u_w, tm=tm)

    lam_init = _lambda_init(layer_idx)
    lam = (jnp.exp(jnp.sum(lambda_q1.astype(F32) * lambda_k1.astype(F32)))
           - jnp.exp(jnp.sum(lambda_q2.astype(F32) * lambda_k2.astype(F32))) + lam_init)
    sub_gain = (subln_g.astype(F32) * (1.0 - lam_init)).reshape(-1, 1)
    score_bound = 1.02 * ATTN_QK_DIM * jnp.max(jnp.abs(q_gain)) * jnp.max(jnp.abs(k_gain))
    ctl = jnp.stack([lam, (score_bound <= ATTN_SCORE_LIMIT).astype(F32)])
    a = _attention(ctl, q, k, vt, sub_gain, t=t)

    nc = seq // SSM_T
    tables = _ssm_tables(
        ssm_a_re.astype(F32), ssm_a_im.astype(F32), ssm_log_dt.astype(F32),
        ssm_b_re.astype(F32), ssm_b_im.astype(F32), ssm_c_re.astype(F32), ssm_c_im.astype(F32),
        ssm_d.astype(F32), SSM_T)
    yt = _ssm(ut.reshape(nc * bsz, u_w, SSM_T), tables, nb=bsz, nc=nc)
    yt = yt.reshape(nc, bsz, u_w, SSM_T)

    return _merge_mlp(
        x, a, yt, gates, w_glu.astype(BF16), row(b_glu),
        w_proj_attn.astype(BF16), w_proj_ssm.astype(BF16), w_out.astype(BF16),
        row(norm_mlp_g), w_mlp_in.astype(BF16), w_mlp_out.astype(BF16),
        tm=tm, ff_tile=MLP_FF_TILE)


def kernel(x, norm_mix_g, w_in, b_gate, q_norm_g, k_norm_g, lambda_q1, lambda_k1, lambda_q2, lambda_k2, subln_g, ssm_a_re, ssm_a_im, ssm_log_dt, ssm_b_re, ssm_b_im, ssm_c_re, ssm_c_im, ssm_d, w_glu, b_glu, w_proj_attn, w_proj_ssm, w_out, norm_mlp_g, w_mlp_in, w_mlp_out):
    for l in range(norm_mix_g.shape[0]):
        x = _layer(
            x, l, norm_mix_g[l], w_in[l], b_gate[l], q_norm_g[l], k_norm_g[l],
            lambda_q1[l], lambda_k1[l], lambda_q2[l], lambda_k2[l], subln_g[l],
            ssm_a_re[l], ssm_a_im[l], ssm_log_dt[l], ssm_b_re[l], ssm_b_im[l],
            ssm_c_re[l], ssm_c_im[l], ssm_d[l], w_glu[l], b_glu[l],
            w_proj_attn[l], w_proj_ssm[l], w_out[l],
            norm_mlp_g[l], w_mlp_in[l], w_mlp_out[l])
    return x
```

```python
import functools
import math

import jax
import jax.numpy as jnp
from jax import lax
from jax.experimental import pallas as pl
from jax.experimental.pallas import tpu as pltpu

F32 = jnp.float32
BF16 = jnp.bfloat16

EPS = 1e-6
CHUNK = 64
ATTN_HEADS = 4
ATTN_QK_DIM = 64
ATTN_V_DIM = 2 * ATTN_QK_DIM
SSM_GROUP = 16
SSM_STATE = 64
LANES = 128
MXU_TILE = 256
SSM_T = LANES
V7X_VMEM_LIMIT_BYTES = 56 * 1024 * 1024

TOKEN_TILE = 512
IN_PROJ_TILE = 1024
ATTN_TILE = 512
ATTN_SCORE_LIMIT = 60.0
MLP_FF_TILE = 1024


def _const_spec(shape):
    zeros = (0,) * len(shape)
    return pl.BlockSpec(shape, lambda *_: zeros, pipeline_mode=pl.Buffered(1))


def _in_proj_kernel(x_ref, g_ref, w_ref, bd_ref, qg_ref, kg_ref, bg_ref,
                    q_ref, k_ref, vt_ref, ut_ref, gate_ref, vu_ref, *, qk_w, v_w, u_w):
    x = x_ref[0]
    ms = jnp.mean(x * x, axis=-1, keepdims=True)
    h = (x * lax.rsqrt(ms + EPS) * g_ref[...]).astype(BF16)

    def proj(lo, hi):
        return jnp.dot(h, w_ref[:, lo:hi], preferred_element_type=F32)

    def group_norm(t, gain):
        sq = (t * t).astype(BF16)
        w = bd_ref.shape[0]
        ms_g = jnp.concatenate(
            [jnp.dot(sq[:, c * w:(c + 1) * w], bd_ref[...], preferred_element_type=F32)
             for c in range(sq.shape[1] // w)], axis=1)
        return t * lax.rsqrt(ms_g + EPS) * gain

    o1 = qk_w
    o2 = o1 + qk_w
    o3 = o2 + v_w
    o4 = o3 + u_w
    q_ref[0] = group_norm(proj(0, o1), qg_ref[...]).astype(BF16)
    k_ref[0] = group_norm(proj(o1, o2), kg_ref[...]).astype(BF16)
    vu_ref[...] = proj(o2, o4)
    tk = vt_ref.shape[-1]
    for hd in range(v_w // ATTN_V_DIM):
        for jk in range(vt_ref.shape[2]):
            vt_ref[0, hd, jk] = vu_ref[jk * tk:(jk + 1) * tk,
                                       hd * ATTN_V_DIM:(hd + 1) * ATTN_V_DIM].T.astype(BF16)
    for j in range(ut_ref.shape[1]):
        ut_ref[:, j, :] = vu_ref[j * SSM_T:(j + 1) * SSM_T, v_w:].T
    gate_ref[0] = jax.nn.sigmoid(proj(o4, w_ref.shape[1]) + bg_ref[...]).astype(BF16)


def _in_proj(x, g, w, bd, qg, kg, bg, *, qk_w, v_w, u_w, tm, tk):
    bsz, seq, d = x.shape
    gate_w = w.shape[1] - 2 * qk_w - v_w - u_w
    heads = v_w // ATTN_V_DIM
    tok = lambda b, i: (b, i, 0)
    out_shapes = (
        jax.ShapeDtypeStruct((bsz, seq, qk_w), BF16),
        jax.ShapeDtypeStruct((bsz, seq, qk_w), BF16),
        jax.ShapeDtypeStruct((bsz, heads, seq // tk, ATTN_V_DIM, tk), BF16),
        jax.ShapeDtypeStruct((u_w, bsz * (seq // SSM_T), SSM_T), F32),
        jax.ShapeDtypeStruct((bsz, seq, gate_w), BF16),
    )
    return pl.pallas_call(
        functools.partial(_in_proj_kernel, qk_w=qk_w, v_w=v_w, u_w=u_w),
        grid=(bsz, seq // tm),
        in_specs=[
            pl.BlockSpec((1, tm, d), tok),
            _const_spec(g.shape), _const_spec(w.shape), _const_spec(bd.shape),
            _const_spec(qg.shape), _const_spec(kg.shape), _const_spec(bg.shape),
        ],
        out_specs=(
            pl.BlockSpec((1, tm, qk_w), tok), pl.BlockSpec((1, tm, qk_w), tok),
            pl.BlockSpec((1, heads, tm // tk, ATTN_V_DIM, tk), lambda b, i: (b, 0, i, 0, 0)),
            pl.BlockSpec((u_w, tm // SSM_T, SSM_T), lambda b, i: (0, b * (seq // tm) + i, 0)),
            pl.BlockSpec((1, tm, gate_w), tok),
        ),
        out_shape=out_shapes,
        scratch_shapes=[pltpu.VMEM((tm, v_w + u_w), F32)],
        compiler_params=pltpu.CompilerParams(
            dimension_semantics=("parallel", "parallel"), vmem_limit_bytes=V7X_VMEM_LIMIT_BYTES),
        name="in_proj",
    )(x, g, w, bd, qg, kg, bg)


def _attn_kernel(ctl_ref, q_ref, k_ref, vt_ref, sg_ref, o_ref,
                 qc_ref, m_ref, l_ref, al_ref, acc_ref, l8_ref, s_ref, p_ref, *, t):
    nq = q_ref.shape[1] // t
    bounded = ctl_ref[1] > 0.0
    lane = lax.broadcasted_iota(jnp.int32, (t, ATTN_V_DIM), 1)

    def masked_queries(q):
        zero = jnp.zeros_like(q)
        return jnp.concatenate([jnp.where(lane < ATTN_QK_DIM, q, zero),
                                jnp.where(lane >= ATTN_QK_DIM, q, zero)], axis=0)

    def raw_scores(kt, qc, masked):
        s = lax.dot_general(kt, qc, (((1,), (1,)), ((), ())),
                            preferred_element_type=F32)
        if masked:
            per_map = qc.shape[0] // 2
            key_chunk = lax.broadcasted_iota(jnp.int32, s.shape, 0) // CHUNK
            qry = lax.broadcasted_iota(jnp.int32, s.shape, 1)
            qry_chunk = jnp.where(qry >= per_map, qry - per_map, qry) // CHUNK
            s = jnp.where(key_chunk <= qry_chunk, s, -jnp.inf)
        return s

    def write_output(row0, acc, l):
        o = acc * (1.0 / l)
        o = o[:, :t] - ctl_ref[0] * o[:, t:]
        ms = jnp.mean(o * o, axis=0, keepdims=True)
        o = o * lax.rsqrt(ms + EPS) * sg_ref[...]
        o_ref[0, pl.ds(row0, t), :] = o.T.astype(BF16)

    @pl.when(bounded)
    def _():
        n = 0
        for i in range(nq):
            qc = masked_queries(q_ref[0, i * t:(i + 1) * t, :])
            a = i % 2
            for j in range(i + 1):
                p = jnp.exp2(raw_scores(k_ref[0, j * t:(j + 1) * t, :], qc, j == i))
                part = jnp.sum(p.reshape(t // 8, 8, 2 * t), axis=0)
                p_ref[n % 2] = p.astype(BF16)
                upd = jnp.dot(vt_ref[0, 0, j], p_ref[n % 2], preferred_element_type=F32)
                if j == 0:
                    acc_ref[a] = upd
                    l8_ref[a] = part
                else:
                    acc_ref[a] += upd
                    l8_ref[a] += part
                n += 1
            write_output(i * t, acc_ref[a], jnp.sum(l8_ref[a], axis=0, keepdims=True))

    def general_tile(i, carry):
        row0 = pl.multiple_of(i * t, t)
        qc_ref[...] = masked_queries(q_ref[0, pl.ds(row0, t), :])
        m_ref[...] = jnp.full(m_ref.shape, -jnp.inf, F32)
        l_ref[...] = jnp.zeros(l_ref.shape, F32)
        al_ref[...] = jnp.ones(al_ref.shape, F32)
        acc_ref[0] = jnp.zeros(acc_ref.shape[1:], F32)
        p_ref[1] = jnp.zeros(p_ref.shape[1:], BF16)

        def scores(j, masked, slot):
            kt = k_ref[0, pl.ds(pl.multiple_of(j * t, t), t), :]
            s_ref[slot] = raw_scores(kt, qc_ref[...], masked)

        def softmax(slot):
            s = s_ref[slot]
            m_old = m_ref[...]
            m_new = jnp.maximum(m_old, jnp.max(s, axis=0, keepdims=True))
            alpha = jnp.exp2(m_old - m_new)
            p = jnp.exp2(s - m_new)
            l_ref[...] = alpha * l_ref[...] + jnp.sum(p, axis=0, keepdims=True)
            m_ref[...] = m_new
            al_ref[...] = alpha
            p_ref[slot] = p.astype(BF16)

        def values(j, slot):
            acc_ref[0] = al_ref[...] * acc_ref[0] + jnp.dot(
                vt_ref[0, 0, jnp.maximum(j, 0)], p_ref[slot],
                preferred_element_type=F32)

        def step(j, slot, masked_next):
            values(j - 1, 1 - slot)
            scores(j + 1, masked_next, 1 - slot)
            softmax(slot)

        def finish(j, slot):
            values(j - 1, 1 - slot)
            softmax(slot)
            values(j, slot)

        @pl.when(i > 0)
        def _():
            scores(0, False, 0)

            def body(h, c):
                step(2 * h, 0, False)
                step(2 * h + 1, 1, False)
                return c

            lax.fori_loop(0, (i - 1) // 2, body, 0)

        @pl.when(i % 2 == 1)
        def _():
            step(i - 1, 0, True)
            finish(i, 1)

        @pl.when(jnp.logical_and(i % 2 == 0, i > 0))
        def _():
            step(i - 2, 0, False)
            step(i - 1, 1, True)
            finish(i, 0)

        @pl.when(i == 0)
        def _():
            scores(0, True, 0)
            finish(0, 0)

        write_output(row0, acc_ref[0], l_ref[...])
        return carry

    @pl.when(jnp.logical_not(bounded))
    def _():
        lax.fori_loop(0, nq, general_tile, 0)


def _attention(ctl, q, k, vt, sg, *, t):
    b, s, w = q.shape
    heads = w // ATTN_V_DIM
    head_cols = lambda bi, hi: (bi, 0, hi)
    return pl.pallas_call(
        functools.partial(_attn_kernel, t=t),
        grid=(b, heads),
        in_specs=[
            pl.BlockSpec(memory_space=pltpu.SMEM),
            pl.BlockSpec((1, s, ATTN_V_DIM), head_cols),
            pl.BlockSpec((1, s, ATTN_V_DIM), head_cols),
            pl.BlockSpec((1, 1, s // t, ATTN_V_DIM, t), lambda bi, hi: (bi, hi, 0, 0, 0)),
            _const_spec(sg.shape),
        ],
        out_specs=pl.BlockSpec((1, s, ATTN_V_DIM), head_cols),
        out_shape=jax.ShapeDtypeStruct((b, s, w), BF16),
        scratch_shapes=[
            pltpu.VMEM((2 * t, ATTN_V_DIM), BF16),
            pltpu.VMEM((1, 2 * t), F32),
            pltpu.VMEM((1, 2 * t), F32),
            pltpu.VMEM((1, 2 * t), F32),
            pltpu.VMEM((2, ATTN_V_DIM, 2 * t), F32),
            pltpu.VMEM((2, 8, 2 * t), F32),
            pltpu.VMEM((2, t, 2 * t), F32),
            pltpu.VMEM((2, t, 2 * t), BF16),
        ],
        compiler_params=pltpu.CompilerParams(
            dimension_semantics=("parallel", "parallel"),
            vmem_limit_bytes=V7X_VMEM_LIMIT_BYTES),
        name="diff_attn",
    )(ctl, q, k, vt, sg)


def _ssm_kernel(*refs, nb, nc):
    ng = SSM_GROUP
    (u_ref, ca_ref, cb_ref, ba_ref, bb_ref, bas_ref, bbs_ref, b2_ref, d_ref,
     pa0_ref, pb0_ref, pa1_ref, pb1_ref, qa_ref, qb_ref, lam_ref,
     y_ref, toep_ref, xc_ref, xcs_ref, x0_ref) = refs
    t = SSM_T
    gw = ng * t

    def rep(ref):
        v = ref[0]
        return jnp.broadcast_to(v[:, None, :], (ng, t, v.shape[-1])).reshape(gw, v.shape[-1])

    def tile(ref):
        v = ref[0]
        return jnp.broadcast_to(v[None], (ng, t, v.shape[-1])).reshape(gw, v.shape[-1])

    rca, rcb = rep(ca_ref), rep(cb_ref)
    cl0 = rca * tile(pa0_ref) + rcb * tile(pb0_ref)
    hm = (rca * tile(pa1_ref) + rcb * tile(pb1_ref)).astype(BF16)
    tqa, tqb = tile(qa_ref), tile(qb_ref)
    gm = (rep(ba_ref) * tqa + rep(bb_ref) * tqb).astype(BF16)
    gms = (rep(bas_ref) * tqa + rep(bbs_ref) * tqb).astype(BF16)

    kr = lax.dot_general(b2_ref[0], cl0, (((1,), (1,)), ((), ())),
                         precision=lax.Precision.HIGHEST, preferred_element_type=F32)
    lane = lax.broadcasted_iota(jnp.int32, kr.shape, 1)
    rowi = lax.broadcasted_iota(jnp.int32, kr.shape, 0)
    dfull = jnp.concatenate([d_ref[0]] * ng, axis=1)
    kr = kr + jnp.where(lane == rowi * t, dfull, 0.0)

    u = jnp.concatenate([u_ref[n] for n in range(ng)], axis=1).astype(BF16)

    causal = (lax.broadcasted_iota(jnp.int32, (t, t), 1)
              >= lax.broadcasted_iota(jnp.int32, (t, t), 0))
    y = None
    for pair in range(ng // 2):
        for n_in in (2 * pair, 2 * pair + 1):
            for n_out in range(ng):
                seg = jnp.broadcast_to(kr[n_in:n_in + 1, n_out * t:(n_out + 1) * t], (t, t))
                blk = pltpu.roll(seg, 0, 1, stride=1, stride_axis=0)
                toep_ref[n_in * t:(n_in + 1) * t, n_out * t:(n_out + 1) * t] = (
                    jnp.where(causal, blk, 0.0).astype(BF16))
        lo, hi = 2 * pair * t, (2 * pair + 2) * t
        part = jnp.dot(u[:, lo:hi], toep_ref[lo:hi, :], preferred_element_type=F32)
        y = part if y is None else y + part
    xcc = jnp.dot(u, jnp.concatenate([gm, gms], axis=1), preferred_element_type=F32)
    xc_ref[...] = xcc[:, :2 * SSM_STATE]
    xcs_ref[...] = xcc[:, 2 * SSM_STATE:]

    lam = lam_ref[0]
    a, b, c = lam[0:1], lam[1:2], lam[2:3]

    def body(ci, carry):
        st, st_sw = carry
        r = pl.ds(ci, nb, stride=nc)
        x0_ref[r, :] = st
        return (st * a + st_sw * b + xc_ref[r, :],
                st_sw * a + st * c + xcs_ref[r, :])

    z = jnp.zeros((nb, 2 * SSM_STATE), F32)
    lax.fori_loop(0, nc, body, (z, z))

    x0 = x0_ref[...]
    x0_hi = x0.astype(BF16)
    x0_lo = (x0 - x0_hi.astype(F32)).astype(BF16)
    nt = (((1,), (1,)), ((), ()))
    y = y + lax.dot_general(jnp.concatenate([x0_hi, x0_lo], axis=1),
                            jnp.concatenate([hm, hm], axis=1), nt, preferred_element_type=F32)
    for n_out in range(ng):
        y_ref[:, n_out, :] = y[:, n_out * t:(n_out + 1) * t]


def _ssm(ut, tables, *, nb, nc):
    width, rows, t = ut.shape
    ng = SSM_GROUP
    groups = width // ng
    st2 = 2 * SSM_STATE
    gw = ng * t
    grp_spec = pl.BlockSpec((rows, ng, t), lambda g: (0, g, 0))
    tab_specs = [pl.BlockSpec((1,) + tab.shape[1:], lambda g: (g, 0, 0)) for tab in tables]
    return pl.pallas_call(
        functools.partial(_ssm_kernel, nb=nb, nc=nc),
        grid=(groups,),
        in_specs=[pl.BlockSpec((ng, rows, t), lambda g: (g, 0, 0))] + tab_specs,
        out_specs=grp_spec,
        out_shape=jax.ShapeDtypeStruct((rows, width, t), F32),
        scratch_shapes=[
            pltpu.VMEM((gw, gw), BF16),
            pltpu.VMEM((rows, st2), F32),
            pltpu.VMEM((rows, st2), F32),
            pltpu.VMEM((rows, st2), F32),
        ],
        compiler_params=pltpu.CompilerParams(
            dimension_semantics=("parallel",), vmem_limit_bytes=V7X_VMEM_LIMIT_BYTES),
        name="s5_ssm",
    )(ut, *tables)


def _ssm_tables(a_re, a_im, log_dt, b_re, b_im, c_re, c_im, d, t):
    dt = jnp.exp(log_dt)[:, None]
    lre = a_re * dt
    ang = a_im * dt
    mag = jnp.exp(lre)
    lb_re, lb_im = mag * jnp.cos(ang), mag * jnp.sin(ang)
    den = a_re * a_re + a_im * a_im
    nr, ni = lb_re - 1.0, lb_im
    f_re = (nr * a_re + ni * a_im) / den
    f_im = (ni * a_re - nr * a_im) / den
    bb_re = f_re[..., None] * b_re - f_im[..., None] * b_im
    bb_im = f_re[..., None] * b_im + f_im[..., None] * b_re
    bt_re, bt_im = jnp.swapaxes(bb_re, 1, 2), jnp.swapaxes(bb_im, 1, 2)
    cat = lambda x, y: jnp.concatenate([x, y], axis=-1)
    k = jnp.arange(t + 1, dtype=F32)[None, :, None]
    pmag = jnp.exp(k * lre[:, None, :])
    p_re, p_im = pmag * jnp.cos(k * ang[:, None, :]), pmag * jnp.sin(k * ang[:, None, :])
    pa, pb = cat(p_re, p_re), cat(p_im, p_im)
    tr, ti = p_re[:, t], p_im[:, t]
    lam_t = jnp.stack([cat(tr, tr), cat(-ti, ti), cat(ti, -ti)], axis=1)
    lam_t = jnp.concatenate([lam_t, jnp.zeros((lam_t.shape[0], 5, lam_t.shape[2]), F32)], axis=1)
    g = a_re.shape[0]
    d_tab = jnp.broadcast_to(d.reshape(g, SSM_GROUP, 1), (g, SSM_GROUP, t))
    return (
        cat(c_re, -c_im), cat(-c_im, -c_re),
        cat(bt_re, bt_im), cat(-bt_im, bt_re),
        cat(bt_im, bt_re), cat(bt_re, -bt_im),
        cat(bt_re, bt_im),
        d_tab,
        pa[:, :t], pb[:, :t],
        pa[:, 1:], pb[:, 1:],
        pa[:, t - 1::-1][:, :t], pb[:, t - 1::-1][:, :t],
        lam_t,
    )


def _merge_mlp_kernel(x_ref, a_ref, yt_ref, gate_ref, wglu_ref, bglu_ref, wpa_ref, wps_ref,
                      wout_ref, gm_ref, wmi_ref, wmo_ref, o_ref, *, ff_tile):
    d = x_ref.shape[2]
    y = jnp.concatenate([yt_ref[j].T for j in range(yt_ref.shape[0])], axis=0)
    z = jax.nn.gelu(y, approximate=True)
    glu = jnp.dot(z.astype(BF16), wglu_ref[...], preferred_element_type=F32) + bglu_ref[...]
    s = z * jax.nn.sigmoid(glu)
    pa = jnp.dot(a_ref[0], wpa_ref[...], preferred_element_type=F32)
    ps = jnp.dot(s.astype(BF16), wps_ref[...], preferred_element_type=F32)
    merged = gate_ref[0, :, :d].astype(F32) * pa + gate_ref[0, :, d:].astype(F32) * ps
    x1 = x_ref[0] + jnp.dot(merged.astype(BF16), wout_ref[...], preferred_element_type=F32)
    ms = jnp.mean(x1 * x1, axis=-1, keepdims=True)
    hm = (x1 * lax.rsqrt(ms + EPS) * gm_ref[...]).astype(BF16)
    acc = x1
    for j in range(wmi_ref.shape[1] // ff_tile):
        hj = jnp.dot(hm, wmi_ref[:, j * ff_tile:(j + 1) * ff_tile], preferred_element_type=F32)
        hj = jnp.square(jnp.maximum(hj, 0.0)).astype(BF16)
        acc = acc + jnp.dot(hj, wmo_ref[j * ff_tile:(j + 1) * ff_tile, :],
                            preferred_element_type=F32)
    o_ref[0] = acc


def _merge_mlp(x, a, yt, gates, wglu, bglu, wpa, wps, wout, gmlp, wmi, wmo, *, tm, ff_tile):
    bsz, seq, d = x.shape
    tok = lambda b, i: (b, i, 0)
    consts = (wglu, bglu, wpa, wps, wout, gmlp, wmi, wmo)
    return pl.pallas_call(
        functools.partial(_merge_mlp_kernel, ff_tile=ff_tile),
        grid=(bsz, seq // tm),
        in_specs=[
            pl.BlockSpec((1, tm, d), tok), pl.BlockSpec((1, tm, a.shape[2]), tok),
            pl.BlockSpec((None, tm // SSM_T, yt.shape[2], SSM_T), lambda b, i: (b, i, 0, 0)),
            pl.BlockSpec((1, tm, gates.shape[2]), tok),
        ] + [_const_spec(c.shape) for c in consts],
        out_specs=pl.BlockSpec((1, tm, d), tok),
        out_shape=jax.ShapeDtypeStruct((bsz, seq, d), F32),
        compiler_params=pltpu.CompilerParams(
            dimension_semantics=("parallel", "parallel"), vmem_limit_bytes=V7X_VMEM_LIMIT_BYTES),
        name="merge_mlp",
    )(x, a, yt, gates, *consts)


def _lambda_init(layer_idx):
    return 0.8 - 0.6 * math.exp(-0.3 * layer_idx)


def _layer(x, layer_idx, norm_mix_g, w_in, b_gate, q_norm_g, k_norm_g, lambda_q1, lambda_k1,
           lambda_q2, lambda_k2, subln_g, ssm_a_re, ssm_a_im, ssm_log_dt, ssm_b_re, ssm_b_im,
           ssm_c_re, ssm_c_im, ssm_d, w_glu, b_glu, w_proj_attn, w_proj_ssm, w_out,
           norm_mlp_g, w_mlp_in, w_mlp_out):
    bsz, seq, d = x.shape
    qk_w = ATTN_HEADS * 2 * ATTN_QK_DIM
    v_w = ATTN_HEADS * ATTN_V_DIM
    u_w = ssm_d.shape[0]
    row = lambda v: v.reshape(1, -1).astype(F32)
    tm = min(TOKEN_TILE, seq)
    tm_in = min(IN_PROJ_TILE, seq)
    t = min(ATTN_TILE, seq)
    assert seq % tm_in == 0 and tm_in % t == 0 and seq % tm == 0 and tm % SSM_T == 0

    group_mean = jnp.kron(jnp.eye(MXU_TILE // ATTN_QK_DIM, dtype=F32),
                          jnp.full((ATTN_QK_DIM, ATTN_QK_DIM), 1.0 / ATTN_QK_DIM, F32)).astype(BF16)
    reps = qk_w // ATTN_QK_DIM
    q_gain = row(jnp.tile(q_norm_g.astype(F32) * (ATTN_QK_DIM ** -0.5 * math.log2(math.e)), reps))
    k_gain = row(jnp.tile(k_norm_g.astype(F32), reps))
    q, k, vt, ut, gates = _in_proj(
        x, row(norm_mix_g), w_in.astype(BF16), group_mean, q_gain, k_gain, row(b_gate),
        qk_w=qk_w, v_w=v_w, u_w=u_w, tm=tm_in, tk=t)

    lam_init = _lambda_init(layer_idx)
    lam = (jnp.exp(jnp.sum(lambda_q1.astype(F32) * lambda_k1.astype(F32)))
           - jnp.exp(jnp.sum(lambda_q2.astype(F32) * lambda_k2.astype(F32))) + lam_init)
    sub_gain = (subln_g.astype(F32) * (1.0 - lam_init)).reshape(-1, 1)
    score_bound = 1.02 * ATTN_QK_DIM * jnp.max(jnp.abs(q_gain)) * jnp.max(jnp.abs(k_gain))
    ctl = jnp.stack([lam, (score_bound <= ATTN_SCORE_LIMIT).astype(F32)])
    a = _attention(ctl, q, k, vt, sub_gain, t=t)

    nc = seq // SSM_T
    tables = _ssm_tables(
        ssm_a_re.astype(F32), ssm_a_im.astype(F32), ssm_log_dt.astype(F32),
        ssm_b_re.astype(F32), ssm_b_im.astype(F32), ssm_c_re.astype(F32), ssm_c_im.astype(F32),
        ssm_d.astype(F32), SSM_T)
    yt = _ssm(ut, tables, nb=bsz, nc=nc)
    yt = yt.reshape(bsz, nc, u_w, SSM_T)

    return _merge_mlp(
        x, a, yt, gates, w_glu.astype(BF16), row(b_glu),
        w_proj_attn.astype(BF16), w_proj_ssm.astype(BF16), w_out.astype(BF16),
        row(norm_mlp_g), w_mlp_in.astype(BF16), w_mlp_out.astype(BF16),
        tm=tm, ff_tile=MLP_FF_TILE)


def kernel(x, norm_mix_g, w_in, b_gate, q_norm_g, k_norm_g, lambda_q1, lambda_k1, lambda_q2, lambda_k2, subln_g, ssm_a_re, ssm_a_im, ssm_log_dt, ssm_b_re, ssm_b_im, ssm_c_re, ssm_c_im, ssm_d, w_glu, b_glu, w_proj_attn, w_proj_ssm, w_out, norm_mlp_g, w_mlp_in, w_mlp_out):
    for l in range(norm_mix_g.shape[0]):
        x = _layer(
            x, l, norm_mix_g[l], w_in[l], b_gate[l], q_norm_g[l], k_norm_g[l],
            lambda_q1[l], lambda_k1[l], lambda_q2[l], lambda_k2[l], subln_g[l],
            ssm_a_re[l], ssm_a_im[l], ssm_log_dt[l], ssm_b_re[l], ssm_b_im[l],
            ssm_c_re[l], ssm_c_im[l], ssm_d[l], w_glu[l], b_glu[l],
            w_proj_attn[l], w_proj_ssm[l], w_out[l],
            norm_mlp_g[l], w_mlp_in[l], w_mlp_out[l])
    return x
```

```python
import functools
import math

import jax
import jax.numpy as jnp
from jax import lax
from jax.experimental import pallas as pl
from jax.experimental.pallas import tpu as pltpu

F32 = jnp.float32
BF16 = jnp.bfloat16

EPS = 1e-6
CHUNK = 64
ATTN_HEADS = 4
ATTN_QK_DIM = 64
ATTN_V_DIM = 2 * ATTN_QK_DIM
SSM_GROUP = 16
SSM_STATE = 64
LANES = 128
SUBLANES = 8
MXU_TILE = 256
SSM_T = LANES
V7X_VMEM_LIMIT_BYTES = 56 * 1024 * 1024

TOKEN_TILE = 512
IN_PROJ_TILE = 1024
ATTN_TILE = 512
ATTN_SCORE_LIMIT = 60.0
MLP_FF_TILE = 1024


def _const_spec(shape):
    zeros = (0,) * len(shape)
    return pl.BlockSpec(shape, lambda *_: zeros, pipeline_mode=pl.Buffered(1))


def _in_proj_kernel(x_ref, g_ref, w_ref, bd_ref, qg_ref, kg_ref, bg_ref,
                    q_ref, k_ref, vt_ref, ut_ref, gate_ref, vu_ref, *, qk_w, v_w, u_w):
    x = x_ref[0]
    ms = jnp.mean(x * x, axis=-1, keepdims=True)
    h = (x * lax.rsqrt(ms + EPS) * g_ref[...]).astype(BF16)

    def proj(lo, hi):
        return jnp.dot(h, w_ref[:, lo:hi], preferred_element_type=F32)

    def group_norm(t, gain):
        sq = (t * t).astype(BF16)
        w = bd_ref.shape[0]
        ms_g = jnp.concatenate(
            [jnp.dot(sq[:, c * w:(c + 1) * w], bd_ref[...], preferred_element_type=F32)
             for c in range(sq.shape[1] // w)], axis=1)
        return t * lax.rsqrt(ms_g + EPS) * gain

    o1 = qk_w
    o2 = o1 + qk_w
    o3 = o2 + v_w
    o4 = o3 + u_w
    q_ref[0] = group_norm(proj(0, o1), qg_ref[...]).astype(BF16)
    k_ref[0] = group_norm(proj(o1, o2), kg_ref[...]).astype(BF16)
    vu_ref[...] = proj(o2, o4)
    tk = vt_ref.shape[-1]
    for hd in range(v_w // ATTN_V_DIM):
        for jk in range(vt_ref.shape[2]):
            vt_ref[0, hd, jk] = vu_ref[jk * tk:(jk + 1) * tk,
                                       hd * ATTN_V_DIM:(hd + 1) * ATTN_V_DIM].T.astype(BF16)
    for j in range(ut_ref.shape[1]):
        ut_ref[:, j, :] = vu_ref[j * SSM_T:(j + 1) * SSM_T, v_w:].T
    gate_ref[0] = jax.nn.sigmoid(proj(o4, w_ref.shape[1]) + bg_ref[...]).astype(BF16)


def _in_proj(x, g, w, bd, qg, kg, bg, *, qk_w, v_w, u_w, tm, tk):
    bsz, seq, d = x.shape
    gate_w = w.shape[1] - 2 * qk_w - v_w - u_w
    heads = v_w // ATTN_V_DIM
    tok = lambda b, i: (b, i, 0)
    out_shapes = (
        jax.ShapeDtypeStruct((bsz, seq, qk_w), BF16),
        jax.ShapeDtypeStruct((bsz, seq, qk_w), BF16),
        jax.ShapeDtypeStruct((bsz, heads, seq // tk, ATTN_V_DIM, tk), BF16),
        jax.ShapeDtypeStruct((u_w, bsz * (seq // SSM_T), SSM_T), F32),
        jax.ShapeDtypeStruct((bsz, seq, gate_w), BF16),
    )
    return pl.pallas_call(
        functools.partial(_in_proj_kernel, qk_w=qk_w, v_w=v_w, u_w=u_w),
        grid=(bsz, seq // tm),
        in_specs=[
            pl.BlockSpec((1, tm, d), tok),
            _const_spec(g.shape), _const_spec(w.shape), _const_spec(bd.shape),
            _const_spec(qg.shape), _const_spec(kg.shape), _const_spec(bg.shape),
        ],
        out_specs=(
            pl.BlockSpec((1, tm, qk_w), tok), pl.BlockSpec((1, tm, qk_w), tok),
            pl.BlockSpec((1, heads, tm // tk, ATTN_V_DIM, tk), lambda b, i: (b, 0, i, 0, 0)),
            pl.BlockSpec((u_w, tm // SSM_T, SSM_T), lambda b, i: (0, b * (seq // tm) + i, 0)),
            pl.BlockSpec((1, tm, gate_w), tok),
        ),
        out_shape=out_shapes,
        scratch_shapes=[pltpu.VMEM((tm, v_w + u_w), F32)],
        compiler_params=pltpu.CompilerParams(
            dimension_semantics=("parallel", "parallel"), vmem_limit_bytes=V7X_VMEM_LIMIT_BYTES),
        name="in_proj",
    )(x, g, w, bd, qg, kg, bg)


def _attn_kernel(ctl_ref, q_ref, k_ref, vt_ref, sg_ref, o_ref,
                 qc_ref, m_ref, l_ref, al_ref, acc_ref, l8_ref, s_ref, p_ref, *, t):
    nq = q_ref.shape[1] // t
    bounded = ctl_ref[1] > 0.0
    lane = lax.broadcasted_iota(jnp.int32, (t, ATTN_V_DIM), 1)

    def masked_queries(q):
        zero = jnp.zeros_like(q)
        return jnp.concatenate([jnp.where(lane < ATTN_QK_DIM, q, zero),
                                jnp.where(lane >= ATTN_QK_DIM, q, zero)], axis=0)

    def raw_scores(kt, qc, masked):
        s = lax.dot_general(kt, qc, (((1,), (1,)), ((), ())),
                            preferred_element_type=F32)
        if masked:
            per_map = qc.shape[0] // 2
            key_chunk = lax.broadcasted_iota(jnp.int32, s.shape, 0) // CHUNK
            qry = lax.broadcasted_iota(jnp.int32, s.shape, 1)
            qry_chunk = jnp.where(qry >= per_map, qry - per_map, qry) // CHUNK
            s = jnp.where(key_chunk <= qry_chunk, s, -jnp.inf)
        return s

    def write_output(row0, acc, l):
        o = acc * (1.0 / l)
        o = o[:, :t] - ctl_ref[0] * o[:, t:]
        ms = jnp.mean(o * o, axis=0, keepdims=True)
        o = o * lax.rsqrt(ms + EPS) * sg_ref[...]
        o_ref[0, pl.ds(row0, t), :] = o.T.astype(BF16)

    @pl.when(bounded)
    def _():
        n = 0
        for i in range(nq):
            qc = masked_queries(q_ref[0, i * t:(i + 1) * t, :])
            a = i % 2
            for j in range(i + 1):
                p = jnp.exp2(raw_scores(k_ref[0, j * t:(j + 1) * t, :], qc, j == i))
                part = jnp.sum(p.reshape(t // 8, 8, 2 * t), axis=0)
                p_ref[n % 2] = p.astype(BF16)
                upd = jnp.dot(vt_ref[0, 0, j], p_ref[n % 2], preferred_element_type=F32)
                if j == 0:
                    acc_ref[a] = upd
                    l8_ref[a] = part
                else:
                    acc_ref[a] += upd
                    l8_ref[a] += part
                n += 1
            write_output(i * t, acc_ref[a], jnp.sum(l8_ref[a], axis=0, keepdims=True))

    def general_tile(i, carry):
        row0 = pl.multiple_of(i * t, t)
        qc_ref[...] = masked_queries(q_ref[0, pl.ds(row0, t), :])
        m_ref[...] = jnp.full(m_ref.shape, -jnp.inf, F32)
        l_ref[...] = jnp.zeros(l_ref.shape, F32)
        al_ref[...] = jnp.ones(al_ref.shape, F32)
        acc_ref[0] = jnp.zeros(acc_ref.shape[1:], F32)
        p_ref[1] = jnp.zeros(p_ref.shape[1:], BF16)

        def scores(j, masked, slot):
            kt = k_ref[0, pl.ds(pl.multiple_of(j * t, t), t), :]
            s_ref[slot] = raw_scores(kt, qc_ref[...], masked)

        def softmax(slot):
            s = s_ref[slot]
            m_old = m_ref[...]
            m_new = jnp.maximum(m_old, jnp.max(s, axis=0, keepdims=True))
            alpha = jnp.exp2(m_old - m_new)
            p = jnp.exp2(s - m_new)
            l_ref[...] = alpha * l_ref[...] + jnp.sum(p, axis=0, keepdims=True)
            m_ref[...] = m_new
            al_ref[...] = alpha
            p_ref[slot] = p.astype(BF16)

        def values(j, slot):
            acc_ref[0] = al_ref[...] * acc_ref[0] + jnp.dot(
                vt_ref[0, 0, jnp.maximum(j, 0)], p_ref[slot],
                preferred_element_type=F32)

        def step(j, slot, masked_next):
            values(j - 1, 1 - slot)
            scores(j + 1, masked_next, 1 - slot)
            softmax(slot)

        def finish(j, slot):
            values(j - 1, 1 - slot)
            softmax(slot)
            values(j, slot)

        @pl.when(i > 0)
        def _():
            scores(0, False, 0)

            def body(h, c):
                step(2 * h, 0, False)
                step(2 * h + 1, 1, False)
                return c

            lax.fori_loop(0, (i - 1) // 2, body, 0)

        @pl.when(i % 2 == 1)
        def _():
            step(i - 1, 0, True)
            finish(i, 1)

        @pl.when(jnp.logical_and(i % 2 == 0, i > 0))
        def _():
            step(i - 2, 0, False)
            step(i - 1, 1, True)
            finish(i, 0)

        @pl.when(i == 0)
        def _():
            scores(0, True, 0)
            finish(0, 0)

        write_output(row0, acc_ref[0], l_ref[...])
        return carry

    @pl.when(jnp.logical_not(bounded))
    def _():
        lax.fori_loop(0, nq, general_tile, 0)


def _attention(ctl, q, k, vt, sg, *, t):
    b, s, w = q.shape
    heads = w // ATTN_V_DIM
    head_cols = lambda bi, hi: (bi, 0, hi)
    return pl.pallas_call(
        functools.partial(_attn_kernel, t=t),
        grid=(b, heads),
        in_specs=[
            pl.BlockSpec(memory_space=pltpu.SMEM),
            pl.BlockSpec((1, s, ATTN_V_DIM), head_cols),
            pl.BlockSpec((1, s, ATTN_V_DIM), head_cols),
            pl.BlockSpec((1, 1, s // t, ATTN_V_DIM, t), lambda bi, hi: (bi, hi, 0, 0, 0)),
            _const_spec(sg.shape),
        ],
        out_specs=pl.BlockSpec((1, s, ATTN_V_DIM), head_cols),
        out_shape=jax.ShapeDtypeStruct((b, s, w), BF16),
        scratch_shapes=[
            pltpu.VMEM((2 * t, ATTN_V_DIM), BF16),
            pltpu.VMEM((1, 2 * t), F32),
            pltpu.VMEM((1, 2 * t), F32),
            pltpu.VMEM((1, 2 * t), F32),
            pltpu.VMEM((2, ATTN_V_DIM, 2 * t), F32),
            pltpu.VMEM((2, 8, 2 * t), F32),
            pltpu.VMEM((2, t, 2 * t), F32),
            pltpu.VMEM((2, t, 2 * t), BF16),
        ],
        compiler_params=pltpu.CompilerParams(
            dimension_semantics=("parallel", "parallel"),
            vmem_limit_bytes=V7X_VMEM_LIMIT_BYTES),
        name="diff_attn",
    )(ctl, q, k, vt, sg)


def _ssm_kernel(*refs, nb, nc):
    ng = SSM_GROUP
    (u_ref, ca_ref, cb_ref, ba_ref, bb_ref, bas_ref, bbs_ref, b2_ref, d_ref,
     pa0_ref, pb0_ref, pa1_ref, pb1_ref, qa_ref, qb_ref, lam_ref,
     y_ref, toep_ref, xc_ref, xcs_ref, x0_ref) = refs
    t = SSM_T
    gw = ng * t

    def rep(ref):
        v = ref[0]
        return jnp.broadcast_to(v[:, None, :], (ng, t, v.shape[-1])).reshape(gw, v.shape[-1])

    def tile(ref):
        v = ref[0]
        return jnp.broadcast_to(v[None], (ng, t, v.shape[-1])).reshape(gw, v.shape[-1])

    rca, rcb = rep(ca_ref), rep(cb_ref)
    cl0 = rca * tile(pa0_ref) + rcb * tile(pb0_ref)
    hm = (rca * tile(pa1_ref) + rcb * tile(pb1_ref)).astype(BF16)
    tqa, tqb = tile(qa_ref), tile(qb_ref)
    gm = (rep(ba_ref) * tqa + rep(bb_ref) * tqb).astype(BF16)
    gms = (rep(bas_ref) * tqa + rep(bbs_ref) * tqb).astype(BF16)

    kr = lax.dot_general(b2_ref[0], cl0, (((1,), (1,)), ((), ())),
                         precision=lax.Precision.HIGHEST, preferred_element_type=F32)
    lane = lax.broadcasted_iota(jnp.int32, kr.shape, 1)
    rowi = lax.broadcasted_iota(jnp.int32, kr.shape, 0)
    dfull = jnp.concatenate([d_ref[0]] * ng, axis=1)
    kr = kr + jnp.where(lane == rowi * t, dfull, 0.0)

    u = jnp.concatenate([u_ref[n] for n in range(ng)], axis=1).astype(BF16)

    causal = (lax.broadcasted_iota(jnp.int32, (t, t), 1)
              >= lax.broadcasted_iota(jnp.int32, (t, t), 0))
    y = None
    for pair in range(ng // 2):
        for n_in in (2 * pair, 2 * pair + 1):
            for n_out in range(ng):
                seg = jnp.broadcast_to(kr[n_in:n_in + 1, n_out * t:(n_out + 1) * t], (t, t))
                blk = pltpu.roll(seg, 0, 1, stride=1, stride_axis=0)
                toep_ref[n_in * t:(n_in + 1) * t, n_out * t:(n_out + 1) * t] = (
                    jnp.where(causal, blk, 0.0).astype(BF16))
        lo, hi = 2 * pair * t, (2 * pair + 2) * t
        part = jnp.dot(u[:, lo:hi], toep_ref[lo:hi, :], preferred_element_type=F32)
        y = part if y is None else y + part
    xcc = jnp.dot(u, jnp.concatenate([gm, gms], axis=1), preferred_element_type=F32)
    xc_ref[...] = xcc[:, :2 * SSM_STATE]
    xcs_ref[...] = xcc[:, 2 * SSM_STATE:]

    lam = lam_ref[0]
    a, b, c = lam[0:1], lam[1:2], lam[2:3]

    def body(ci, carry):
        st, st_sw = carry
        r = pl.ds(ci, nb, stride=nc)
        x0_ref[r, :] = st
        return (st * a + st_sw * b + xc_ref[r, :],
                st_sw * a + st * c + xcs_ref[r, :])

    z = jnp.zeros((nb, 2 * SSM_STATE), F32)
    lax.fori_loop(0, nc, body, (z, z))

    x0 = x0_ref[...]
    x0_hi = x0.astype(BF16)
    x0_lo = (x0 - x0_hi.astype(F32)).astype(BF16)
    nt = (((1,), (1,)), ((), ()))
    y = y + lax.dot_general(jnp.concatenate([x0_hi, x0_lo], axis=1),
                            jnp.concatenate([hm, hm], axis=1), nt, preferred_element_type=F32)
    for n_out in range(ng):
        y_ref[n_out] = y[:, n_out * t:(n_out + 1) * t]


def _ssm(ut, tables, *, nb, nc):
    width, rows, t = ut.shape
    ng = SSM_GROUP
    groups = width // ng
    st2 = 2 * SSM_STATE
    gw = ng * t
    grp_spec = pl.BlockSpec((ng, rows, t), lambda g: (g, 0, 0))
    tab_specs = [pl.BlockSpec((1,) + tab.shape[1:], lambda g: (g, 0, 0)) for tab in tables]
    return pl.pallas_call(
        functools.partial(_ssm_kernel, nb=nb, nc=nc),
        grid=(groups,),
        in_specs=[grp_spec] + tab_specs,
        out_specs=grp_spec,
        out_shape=jax.ShapeDtypeStruct((width, rows, t), F32),
        scratch_shapes=[
            pltpu.VMEM((gw, gw), BF16),
            pltpu.VMEM((rows, st2), F32),
            pltpu.VMEM((rows, st2), F32),
            pltpu.VMEM((rows, st2), F32),
        ],
        compiler_params=pltpu.CompilerParams(
            dimension_semantics=("parallel",), vmem_limit_bytes=V7X_VMEM_LIMIT_BYTES),
        name="s5_ssm",
    )(ut, *tables)


def _ssm_tables(a_re, a_im, log_dt, b_re, b_im, c_re, c_im, d, t):
    dt = jnp.exp(log_dt)[:, None]
    lre = a_re * dt
    ang = a_im * dt
    mag = jnp.exp(lre)
    lb_re, lb_im = mag * jnp.cos(ang), mag * jnp.sin(ang)
    den = a_re * a_re + a_im * a_im
    nr, ni = lb_re - 1.0, lb_im
    f_re = (nr * a_re + ni * a_im) / den
    f_im = (ni * a_re - nr * a_im) / den
    bb_re = f_re[..., None] * b_re - f_im[..., None] * b_im
    bb_im = f_re[..., None] * b_im + f_im[..., None] * b_re
    bt_re, bt_im = jnp.swapaxes(bb_re, 1, 2), jnp.swapaxes(bb_im, 1, 2)
    cat = lambda x, y: jnp.concatenate([x, y], axis=-1)
    k = jnp.arange(t + 1, dtype=F32)[None, :, None]
    pmag = jnp.exp(k * lre[:, None, :])
    p_re, p_im = pmag * jnp.cos(k * ang[:, None, :]), pmag * jnp.sin(k * ang[:, None, :])
    pa, pb = cat(p_re, p_re), cat(p_im, p_im)
    tr, ti = p_re[:, t], p_im[:, t]
    lam_t = jnp.stack([cat(tr, tr), cat(-ti, ti), cat(ti, -ti)], axis=1)
    lam_t = jnp.concatenate([lam_t, jnp.zeros((lam_t.shape[0], 5, lam_t.shape[2]), F32)], axis=1)
    g = a_re.shape[0]
    d_tab = jnp.broadcast_to(d.reshape(g, SSM_GROUP, 1), (g, SSM_GROUP, t))
    return (
        cat(c_re, -c_im), cat(-c_im, -c_re),
        cat(bt_re, bt_im), cat(-bt_im, bt_re),
        cat(bt_im, bt_re), cat(bt_re, -bt_im),
        cat(bt_re, bt_im),
        d_tab,
        pa[:, :t], pb[:, :t],
        pa[:, 1:], pb[:, 1:],
        pa[:, t - 1::-1][:, :t], pb[:, t - 1::-1][:, :t],
        lam_t,
    )


def _merge_mlp_kernel(x_ref, a_ref, yt_ref, gate_ref, wglu_ref, bglu_ref, wpa_ref, wps_ref,
                      wout_ref, gm_ref, wmi_ref, wmo_ref, o_ref, *, ff_tile):
    _, tm, d = x_ref.shape
    ch, blocks, _ = yt_ref.shape
    nblk = tm // SSM_T
    first = (pl.program_id(1) * nblk) % blocks
    yt_rows = yt_ref.reshape(ch * blocks, SSM_T)
    y = jnp.concatenate([yt_rows[pl.ds(first + j, ch, stride=blocks), :].T
                         for j in range(nblk)], axis=0)
    z = jax.nn.gelu(y, approximate=True)
    glu = jnp.dot(z.astype(BF16), wglu_ref[...], preferred_element_type=F32) + bglu_ref[...]
    s = z * jax.nn.sigmoid(glu)
    pa = jnp.dot(a_ref[0], wpa_ref[...], preferred_element_type=F32)
    ps = jnp.dot(s.astype(BF16), wps_ref[...], preferred_element_type=F32)
    merged = gate_ref[0, :, :d].astype(F32) * pa + gate_ref[0, :, d:].astype(F32) * ps
    x1 = x_ref[0] + jnp.dot(merged.astype(BF16), wout_ref[...], preferred_element_type=F32)
    ms = jnp.mean(x1 * x1, axis=-1, keepdims=True)
    hm = (x1 * lax.rsqrt(ms + EPS) * gm_ref[...]).astype(BF16)
    acc = x1
    for j in range(wmi_ref.shape[1] // ff_tile):
        hj = jnp.dot(hm, wmi_ref[:, j * ff_tile:(j + 1) * ff_tile], preferred_element_type=F32)
        hj = jnp.square(jnp.maximum(hj, 0.0)).astype(BF16)
        acc = acc + jnp.dot(hj, wmo_ref[j * ff_tile:(j + 1) * ff_tile, :],
                            preferred_element_type=F32)
    o_ref[0] = acc


def _merge_mlp(x, a, yt, gates, wglu, bglu, wpa, wps, wout, gmlp, wmi, wmo, *, tm, ff_tile):
    bsz, seq, d = x.shape
    tok = lambda b, i: (b, i, 0)
    consts = (wglu, bglu, wpa, wps, wout, gmlp, wmi, wmo)
    return pl.pallas_call(
        functools.partial(_merge_mlp_kernel, ff_tile=ff_tile),
        grid=(bsz, seq // tm),
        in_specs=[
            pl.BlockSpec((1, tm, d), tok), pl.BlockSpec((1, tm, a.shape[2]), tok),
            pl.BlockSpec((yt.shape[0], SUBLANES, SSM_T),
                         lambda b, i: (0, (b * (seq // SSM_T) + i * (tm // SSM_T)) // SUBLANES, 0)),
            pl.BlockSpec((1, tm, gates.shape[2]), tok),
        ] + [_const_spec(c.shape) for c in consts],
        out_specs=pl.BlockSpec((1, tm, d), tok),
        out_shape=jax.ShapeDtypeStruct((bsz, seq, d), F32),
        compiler_params=pltpu.CompilerParams(
            dimension_semantics=("parallel", "parallel"), vmem_limit_bytes=V7X_VMEM_LIMIT_BYTES),
        name="merge_mlp",
    )(x, a, yt, gates, *consts)


def _lambda_init(layer_idx):
    return 0.8 - 0.6 * math.exp(-0.3 * layer_idx)


def _layer(x, layer_idx, norm_mix_g, w_in, b_gate, q_norm_g, k_norm_g, lambda_q1, lambda_k1,
           lambda_q2, lambda_k2, subln_g, ssm_a_re, ssm_a_im, ssm_log_dt, ssm_b_re, ssm_b_im,
           ssm_c_re, ssm_c_im, ssm_d, w_glu, b_glu, w_proj_attn, w_proj_ssm, w_out,
           norm_mlp_g, w_mlp_in, w_mlp_out):
    bsz, seq, d = x.shape
    qk_w = ATTN_HEADS * 2 * ATTN_QK_DIM
    v_w = ATTN_HEADS * ATTN_V_DIM
    u_w = ssm_d.shape[0]
    row = lambda v: v.reshape(1, -1).astype(F32)
    tm = min(TOKEN_TILE, seq)
    tm_in = min(IN_PROJ_TILE, seq)
    t = min(ATTN_TILE, seq)
    assert seq % tm_in == 0 and tm_in % t == 0 and seq % tm == 0 and tm % SSM_T == 0

    group_mean = jnp.kron(jnp.eye(MXU_TILE // ATTN_QK_DIM, dtype=F32),
                          jnp.full((ATTN_QK_DIM, ATTN_QK_DIM), 1.0 / ATTN_QK_DIM, F32)).astype(BF16)
    reps = qk_w // ATTN_QK_DIM
    q_gain = row(jnp.tile(q_norm_g.astype(F32) * (ATTN_QK_DIM ** -0.5 * math.log2(math.e)), reps))
    k_gain = row(jnp.tile(k_norm_g.astype(F32), reps))
    q, k, vt, ut, gates = _in_proj(
        x, row(norm_mix_g), w_in.astype(BF16), group_mean, q_gain, k_gain, row(b_gate),
        qk_w=qk_w, v_w=v_w, u_w=u_w, tm=tm_in, tk=t)

    lam_init = _lambda_init(layer_idx)
    lam = (jnp.exp(jnp.sum(lambda_q1.astype(F32) * lambda_k1.astype(F32)))
           - jnp.exp(jnp.sum(lambda_q2.astype(F32) * lambda_k2.astype(F32))) + lam_init)
    sub_gain = (subln_g.astype(F32) * (1.0 - lam_init)).reshape(-1, 1)
    score_bound = 1.02 * ATTN_QK_DIM * jnp.max(jnp.abs(q_gain)) * jnp.max(jnp.abs(k_gain))
    ctl = jnp.stack([lam, (score_bound <= ATTN_SCORE_LIMIT).astype(F32)])
    a = _attention(ctl, q, k, vt, sub_gain, t=t)

    nc = seq // SSM_T
    tables = _ssm_tables(
        ssm_a_re.astype(F32), ssm_a_im.astype(F32), ssm_log_dt.astype(F32),
        ssm_b_re.astype(F32), ssm_b_im.astype(F32), ssm_c_re.astype(F32), ssm_c_im.astype(F32),
        ssm_d.astype(F32), SSM_T)
    yt = _ssm(ut, tables, nb=bsz, nc=nc)

    return _merge_mlp(
        x, a, yt, gates, w_glu.astype(BF16), row(b_glu),
        w_proj_attn.astype(BF16), w_proj_ssm.astype(BF16), w_out.astype(BF16),
        row(norm_mlp_g), w_mlp_in.astype(BF16), w_mlp_out.astype(BF16),
        tm=tm, ff_tile=MLP_FF_TILE)


def kernel(x, norm_mix_g, w_in, b_gate, q_norm_g, k_norm_g, lambda_q1, lambda_k1, lambda_q2, lambda_k2, subln_g, ssm_a_re, ssm_a_im, ssm_log_dt, ssm_b_re, ssm_b_im, ssm_c_re, ssm_c_im, ssm_d, w_glu, b_glu, w_proj_attn, w_proj_ssm, w_out, norm_mlp_g, w_mlp_in, w_mlp_out):
    for l in range(norm_mix_g.shape[0]):
        x = _layer(
            x, l, norm_mix_g[l], w_in[l], b_gate[l], q_norm_g[l], k_norm_g[l],
            lambda_q1[l], lambda_k1[l], lambda_q2[l], lambda_k2[l], subln_g[l],
            ssm_a_re[l], ssm_a_im[l], ssm_log_dt[l], ssm_b_re[l], ssm_b_im[l],
            ssm_c_re[l], ssm_c_im[l], ssm_d[l], w_glu[l], b_glu[l],
            w_proj_attn[l], w_proj_ssm[l], w_out[l],
            norm_mlp_g[l], w_mlp_in[l], w_mlp_out[l])
    return x
```
